```python
import jax, jax.numpy as jnp
from jax import lax
import numpy as np

D_MODEL = 1024
BATCH = 4
SEQ = 4096
DEPTH = 1
DEC_BATCH = 8
DEC_SEQ = 8192
PAST_LEN = 128

HEAD_DIM = 64
A_Q_HEADS = 8
A_KV_HEADS = 2
B_Q_HEADS = 8
B_KV_HEADS = 2
M_HEADS = 4
M_HEAD_DIM = 128
N_MEM = 256
A_Q_W = A_Q_HEADS * HEAD_DIM
A_KV_W = A_KV_HEADS * HEAD_DIM
B_Q_W = B_Q_HEADS * HEAD_DIM
B_KV_W = B_KV_HEADS * HEAD_DIM
M_W = M_HEADS * M_HEAD_DIM
IN_WIDTHS = (A_Q_W, A_KV_W, A_KV_W, B_Q_W, B_KV_W, B_KV_W, M_W, D_MODEL, D_MODEL, D_MODEL)
IN_W = A_Q_W + 2 * A_KV_W + B_Q_W + 2 * B_KV_W + M_W + 3 * D_MODEL
GRID_W = 64
WINDOW = 128
BLOCK = 128
ROPE_THETA = 10000.0
PEER_HEADS = 8
PEER_N_KEYS = 128
PEER_N_EXPERTS = PEER_N_KEYS * PEER_N_KEYS
PEER_QDIM = 256
PEER_TOPK = 16
EPS = 1e-6
NEG_INF = -1e30

kernel_name = 'hybrid_gated_gqa_window_mem_peer_encoder'


def rmsnorm(x, g):
    x32 = x.astype(jnp.float32)
    y = x32 * lax.rsqrt(jnp.mean(x32 * x32, axis=-1, keepdims=True) + EPS)
    return (y * g.astype(jnp.float32)).astype(x.dtype)


def split_heads(t, n, dh):
    b, s, _ = t.shape
    return t.reshape(b, s, n, dh).transpose(0, 2, 1, 3)


def merge_heads(o):
    b, h, s, dh = o.shape
    return o.transpose(0, 2, 1, 3).reshape(b, s, h * dh)


def alibi_slopes(n):
    return 2.0 ** (-8.0 * jnp.arange(1, n + 1, dtype=jnp.float32) / n)


def axial_rope_tables(seq_len):
    rows = seq_len // GRID_W
    row_ids = jnp.repeat(jnp.arange(rows, dtype=jnp.float32), GRID_W)
    col_ids = jnp.tile(jnp.arange(GRID_W, dtype=jnp.float32), rows)
    n_freq = HEAD_DIM // 4
    inv_freq = ROPE_THETA ** (-jnp.arange(n_freq, dtype=jnp.float32) / n_freq)
    ang_r = row_ids[:, None] * inv_freq[None, :]
    ang_c = col_ids[:, None] * inv_freq[None, :]
    return (jnp.cos(ang_r), jnp.sin(ang_r), jnp.cos(ang_c), jnp.sin(ang_c))


def rotate(x, cos, sin):
    n = x.shape[-1] // 2
    x1, x2 = x[..., :n], x[..., n:]
    return jnp.concatenate([x1 * cos - x2 * sin, x1 * sin + x2 * cos], axis=-1)


def axial_rope(x, tabs):
    cos_r, sin_r, cos_c, sin_c = tabs
    xf = x.astype(jnp.float32)
    half = HEAD_DIM // 2
    out = jnp.concatenate([rotate(xf[..., :half], cos_r, sin_r),
                           rotate(xf[..., half:], cos_c, sin_c)], axis=-1)
    return out.astype(x.dtype)


def global_gqa(q, k, v):
    b, hq, s, dh = q.shape
    hkv = k.shape[1]
    g = hq // hkv
    nb = s // BLOCK
    scale = dh ** -0.5
    qb = q.reshape(b, hkv, g, nb, BLOCK, dh).transpose(3, 0, 1, 2, 4, 5)

    def block(qi):
        logits = jnp.einsum('bkgqd,bksd->bkgqs', qi, k).astype(jnp.float32) * scale
        p = jax.nn.softmax(logits, axis=-1).astype(v.dtype)
        return jnp.einsum('bkgqs,bksd->bkgqd', p, v)

    o = lax.map(block, qb)
    return o.transpose(1, 2, 3, 0, 4, 5).reshape(b, hq, s, dh)


def window_gqa_sink(q, k, v, slopes, sink):
    b, hq, s, dh = q.shape
    hkv = k.shape[1]
    g = hq // hkv
    nb = s // BLOCK
    span = BLOCK + 2 * WINDOW
    scale = dh ** -0.5
    qb = q.reshape(b, hkv, g, nb, BLOCK, dh).transpose(3, 0, 1, 2, 4, 5)
    pad = ((0, 0), (0, 0), (WINDOW, WINDOW), (0, 0))
    k_pad = jnp.pad(k, pad)
    v_pad = jnp.pad(v, pad)
    r = jnp.arange(BLOCK)[:, None]
    c = jnp.arange(span)[None, :]
    dist = r + WINDOW - c
    penalty = slopes.reshape(hkv, g, 1, 1) * jnp.abs(dist).astype(jnp.float32)
    sink_l = jnp.broadcast_to(sink.reshape(1, hkv, g, 1, 1).astype(jnp.float32), (b, hkv, g, BLOCK, 1))

    def block(args):
        i, qi = args
        start = i * BLOCK
        ks = lax.dynamic_slice_in_dim(k_pad, start, span, axis=2)
        vs = lax.dynamic_slice_in_dim(v_pad, start, span, axis=2)
        key_pos = start - WINDOW + c
        valid = (jnp.abs(dist) <= WINDOW) & (key_pos >= 0) & (key_pos < s)
        logits = jnp.einsum('bkgqd,bksd->bkgqs', qi, ks).astype(jnp.float32) * scale - penalty
        logits = jnp.where(valid, logits, NEG_INF)
        p = jax.nn.softmax(jnp.concatenate([logits, sink_l], axis=-1), axis=-1)[..., :span]
        return jnp.einsum('bkgqs,bksd->bkgqd', p.astype(v.dtype), vs)

    o = lax.map(block, (jnp.arange(nb), qb))
    return o.transpose(1, 2, 3, 0, 4, 5).reshape(b, hq, s, dh)


def memory_attn(q, mk, mv):
    scale = q.shape[-1] ** -0.5
    logits = jnp.einsum('bhsd,bhmd->bhsm', q, mk).astype(jnp.float32) * scale
    p = jax.nn.softmax(logits, axis=-1).astype(mv.dtype)
    return jnp.einsum('bhsm,bhmd->bhsd', p, mv)


def peer(h, w_pq, keys1, keys2, u_tab, v_tab):
    b, s, d = h.shape
    half = PEER_QDIM // 2

    def block(xb):
        q = (xb @ w_pq).reshape(BLOCK, PEER_HEADS, 2, half)
        s1 = jnp.einsum('thd,hnd->thn', q[:, :, 0], keys1).astype(jnp.float32)
        s2 = jnp.einsum('thd,hnd->thn', q[:, :, 1], keys2).astype(jnp.float32)
        sv1, si1 = lax.top_k(s1, PEER_TOPK)
        sv2, si2 = lax.top_k(s2, PEER_TOPK)
        cand = (sv1[..., :, None] + sv2[..., None, :]).reshape(BLOCK, PEER_HEADS, PEER_TOPK * PEER_TOPK)
        score, ci = lax.top_k(cand, PEER_TOPK)
        e1 = jnp.take_along_axis(si1, ci // PEER_TOPK, axis=-1)
        e2 = jnp.take_along_axis(si2, ci % PEER_TOPK, axis=-1)
        idx = e1 * PEER_N_KEYS + e2
        gate = jax.nn.softmax(score, axis=-1)
        act = jnp.einsum('thkd,td->thk', u_tab[idx], xb).astype(jnp.float32)
        coef = (jax.nn.gelu(act, approximate=False) * gate).astype(xb.dtype)
        return jnp.einsum('thk,thkd->td', coef, v_tab[idx])

    out = lax.map(block, h.reshape(-1, BLOCK, d))
    return out.reshape(b, s, d)


def encoder_layer(x, mem, g_attn, w_in, g_qa, g_ka, sink_b, w_mkv, g_mem, w_pa, w_pb, w_pm, w_o,
                  g_ffn, w_pq, keys1, keys2, u_tab, v_tab):
    s = x.shape[1]
    h = rmsnorm(x, g_attn)
    z = h @ w_in
    offs = np.cumsum(IN_WIDTHS)[:-1].tolist()
    qa, ka, va, qb, kb, vb, qm, ga, gb, gm = jnp.split(z, offs, axis=-1)
    tabs = axial_rope_tables(s)
    qa = axial_rope(rmsnorm(split_heads(qa, A_Q_HEADS, HEAD_DIM), g_qa), tabs)
    ka = axial_rope(rmsnorm(split_heads(ka, A_KV_HEADS, HEAD_DIM), g_ka), tabs)
    o_a = merge_heads(global_gqa(qa, ka, split_heads(va, A_KV_HEADS, HEAD_DIM)))
    o_b = merge_heads(window_gqa_sink(split_heads(qb, B_Q_HEADS, HEAD_DIM),
                                      split_heads(kb, B_KV_HEADS, HEAD_DIM),
                                      split_heads(vb, B_KV_HEADS, HEAD_DIM),
                                      alibi_slopes(B_Q_HEADS), sink_b))
    mk, mv = jnp.split(rmsnorm(mem, g_mem) @ w_mkv, 2, axis=-1)
    o_m = merge_heads(memory_attn(split_heads(qm, M_HEADS, M_HEAD_DIM),
                                  split_heads(mk, M_HEADS, M_HEAD_DIM),
                                  split_heads(mv, M_HEADS, M_HEAD_DIM)))
    merged = (jax.nn.sigmoid(ga) * (o_a @ w_pa) + jax.nn.sigmoid(gb) * (o_b @ w_pb)
              + jax.nn.sigmoid(gm) * (o_m @ w_pm))
    x = x + merged @ w_o
    x = x + peer(rmsnorm(x, g_ffn), w_pq, keys1, keys2, u_tab, v_tab)
    return x


def trunk(x, mem, params, g_final):
    for l in range(DEPTH):
        x = encoder_layer(x, mem, *[p[l] for p in params])
    return rmsnorm(x, g_final)


def setup_inputs(seed: int = 0) -> dict:
    key = jax.random.key(seed)
    ks = jax.random.split(key, 24)
    f32 = jnp.float32

    def nrm(k, shape, scale):
        return jax.random.normal(k, shape, f32) * scale

    def gain(k, shape):
        return 1.0 + 0.02 * jax.random.normal(k, shape, f32)

    half = PEER_QDIM // 2
    return {
        'x_prompt': nrm(ks[0], (BATCH, SEQ, D_MODEL), 1.0),
        'x_sample': nrm(ks[1], (DEC_BATCH, DEC_SEQ, D_MODEL), 1.0),
        'mem_prompt': nrm(ks[2], (BATCH, N_MEM, D_MODEL), 1.0),
        'mem_sample': nrm(ks[3], (DEC_BATCH, N_MEM, D_MODEL), 1.0),
        'g_attn': gain(ks[4], (DEPTH, D_MODEL)),
        'w_in': nrm(ks[5], (DEPTH, D_MODEL, IN_W), D_MODEL ** -0.5),
        'g_qa': gain(ks[6], (DEPTH, HEAD_DIM)),
        'g_ka': gain(ks[7], (DEPTH, HEAD_DIM)),
        'sink_b': nrm(ks[8], (DEPTH, B_Q_HEADS), 0.1),
        'w_mkv': nrm(ks[9], (DEPTH, D_MODEL, 2 * M_W), D_MODEL ** -0.5),
        'g_mem': gain(ks[10], (DEPTH, D_MODEL)),
        'w_pa': nrm(ks[11], (DEPTH, A_Q_W, D_MODEL), A_Q_W ** -0.5),
        'w_pb': nrm(ks[12], (DEPTH, B_Q_W, D_MODEL), B_Q_W ** -0.5),
        'w_pm': nrm(ks[13], (DEPTH, M_W, D_MODEL), M_W ** -0.5),
        'w_o': nrm(ks[14], (DEPTH, D_MODEL, D_MODEL), D_MODEL ** -0.5),
        'g_ffn': gain(ks[15], (DEPTH, D_MODEL)),
        'w_pq': nrm(ks[16], (DEPTH, D_MODEL, PEER_HEADS * PEER_QDIM), D_MODEL ** -0.5),
        'peer_keys1': nrm(ks[17], (DEPTH, PEER_HEADS, PEER_N_KEYS, half), half ** -0.5),
        'peer_keys2': nrm(ks[18], (DEPTH, PEER_HEADS, PEER_N_KEYS, half), half ** -0.5),
        'peer_u': nrm(ks[19], (DEPTH, PEER_N_EXPERTS, D_MODEL), D_MODEL ** -0.5),
        'peer_v': nrm(ks[20], (DEPTH, PEER_N_EXPERTS, D_MODEL), (PEER_HEADS * PEER_TOPK) ** -0.5),
        'g_final': gain(ks[21], (D_MODEL,)),
    }


def reference(x_prompt, x_sample, mem_prompt, mem_sample, g_attn, w_in, g_qa, g_ka, sink_b, w_mkv,
              g_mem, w_pa, w_pb, w_pm, w_o, g_ffn, w_pq, peer_keys1, peer_keys2, peer_u, peer_v,
              g_final):
    params = (g_attn, w_in, g_qa, g_ka, sink_b, w_mkv, g_mem, w_pa, w_pb, w_pm, w_o,
              g_ffn, w_pq, peer_keys1, peer_keys2, peer_u, peer_v)
    y_prompt = trunk(x_prompt, mem_prompt, params, g_final)
    y_sample = trunk(x_sample, mem_sample, params, g_final)
    return (y_prompt, y_sample)
```

```python
import functools

import jax
import jax.numpy as jnp
import numpy as np
from jax import lax
from jax.experimental import pallas as pl
from jax.experimental.pallas import tpu as pltpu

F32 = jnp.float32
BF16 = jnp.bfloat16
I32 = jnp.int32

D_MODEL = 1024
HEAD_DIM = 64
Q_HEADS = 8
KV_HEADS = 2
GROUP = Q_HEADS // KV_HEADS
Q_W = Q_HEADS * HEAD_DIM
KV_W = KV_HEADS * HEAD_DIM
M_HEADS = 4
M_HEAD_DIM = 128
M_W = M_HEADS * M_HEAD_DIM
N_MEM = 256
QKV_W = 2 * (Q_W + 2 * KV_W) + M_W
GATE_W = 3 * D_MODEL
GRID_W = 64
WINDOW = 128
BLOCK = 128
SPAN = BLOCK + 2 * WINDOW
ROPE_THETA = 10000.0
ROPE_HALF = HEAD_DIM // 4
PEER_HEADS = 8
PEER_KEYS = 128
PEER_HALF = 128
PEER_TOPK = 16
PEER_PAIRS = PEER_HEADS * PEER_TOPK
EPS = 1e-6
NEG_INF = -1e30

VMEM_LIMIT_BYTES = 56 * 1024 * 1024
PACK_ROWS = 4
HALF_D = D_MODEL // 2


def _cparams(n_axes):
    return pltpu.CompilerParams(
        dimension_semantics=("arbitrary",) * n_axes,
        vmem_limit_bytes=VMEM_LIMIT_BYTES,
    )


def _rms(x, g):
    ms = jnp.mean(x * x, axis=-1, keepdims=True)
    return (x * lax.rsqrt(ms + EPS)) * g


def _dot(a, b):
    return jnp.dot(a, b, preferred_element_type=F32)


def _dot_nt(a, b):
    return lax.dot_general(a, b, (((1,), (1,)), ((), ())), preferred_element_type=F32)


def _dot_f32_by_01(a, m01):
    hi = a.astype(BF16)
    r1 = a - hi.astype(F32)
    mid = r1.astype(BF16)
    lo = (r1 - mid.astype(F32)).astype(BF16)
    return _dot(lo, m01) + _dot(mid, m01) + _dot(hi, m01)


def _mem_kv_kernel(mem_ref, g_ref, w_ref, mkt_ref, mv_ref):
    h = _rms(mem_ref[0], g_ref[...]).astype(BF16)
    z = _dot(h, w_ref[...])
    mkt_ref[0] = z[:, :M_W].T.astype(BF16)
    mv_ref[0] = z[:, M_W:].astype(BF16)


def _mem_kv(mem, g_mem, w_mkv):
    b = mem.shape[0]
    return pl.pallas_call(
        _mem_kv_kernel,
        grid=(b,),
        in_specs=[
            pl.BlockSpec((1, N_MEM, D_MODEL), lambda i: (i, 0, 0)),
            pl.BlockSpec((1, D_MODEL), lambda i: (0, 0)),
            pl.BlockSpec((D_MODEL, 2 * M_W), lambda i: (0, 0)),
        ],
        out_specs=[
            pl.BlockSpec((1, M_W, N_MEM), lambda i: (i, 0, 0)),
            pl.BlockSpec((1, N_MEM, M_W), lambda i: (i, 0, 0)),
        ],
        out_shape=[
            jax.ShapeDtypeStruct((b, M_W, N_MEM), BF16),
            jax.ShapeDtypeStruct((b, N_MEM, M_W), BF16),
        ],
        compiler_params=_cparams(1),
        name="mem_kv",
    )(mem, g_mem.reshape(1, D_MODEL), w_mkv.astype(BF16))


def _head_norm_rope(q, g, bd, c, s_up, s_dn, width):
    ss = _dot_f32_by_01(q * q, bd)
    qn = (q * lax.rsqrt(ss * (1.0 / HEAD_DIM) + EPS)) * g
    up = pltpu.roll(qn, width - ROPE_HALF, 1)
    dn = pltpu.roll(qn, ROPE_HALF, 1)
    return qn * c + up * s_up + dn * s_dn


def _qkv_kernel(x_ref, g_ref, w_ref, gq_ref, gk_ref, c_ref, su_ref, sd_ref, bdq_ref, bdk_ref,
                qa_ref, kat_ref, va_ref, qb_ref, kb_ref, vb_ref, qm_ref):
    h = _rms(x_ref[0], g_ref[...]).astype(BF16)
    z = _dot(h, w_ref[...])
    c2, su2, sd2 = c_ref[...], su_ref[...], sd_ref[...]
    rep = Q_W // KV_W
    c8 = jnp.concatenate([c2] * rep, axis=1)
    su8 = jnp.concatenate([su2] * rep, axis=1)
    sd8 = jnp.concatenate([sd2] * rep, axis=1)
    o = 0
    qa = _head_norm_rope(z[:, o:o + Q_W], gq_ref[...], bdq_ref[...], c8, su8, sd8, Q_W)
    qa_ref[0] = (qa * (HEAD_DIM ** -0.5)).astype(BF16)
    o += Q_W
    ka = _head_norm_rope(z[:, o:o + KV_W], gk_ref[...], bdk_ref[...], c2, su2, sd2, KV_W)
    kat_ref[0] = ka.T.astype(BF16)
    o += KV_W
    va_ref[0] = z[:, o:o + KV_W].astype(BF16)
    o += KV_W
    qb_ref[0] = (z[:, o:o + Q_W] * (HEAD_DIM ** -0.5)).astype(BF16)
    o += Q_W
    kb_ref[0] = z[:, o:o + KV_W].astype(BF16)
    o += KV_W
    vb_ref[0] = z[:, o:o + KV_W].astype(BF16)
    o += KV_W
    qm_ref[0] = z[:, o:o + M_W].astype(BF16)


def _rope_tables(seq_len):
    rows = seq_len // GRID_W
    row_ids = jnp.repeat(jnp.arange(rows, dtype=F32), GRID_W)
    col_ids = jnp.tile(jnp.arange(GRID_W, dtype=F32), rows)
    inv_freq = ROPE_THETA ** (-jnp.arange(ROPE_HALF, dtype=F32) / ROPE_HALF)
    ang_r = row_ids[:, None] * inv_freq[None, :]
    ang_c = col_ids[:, None] * inv_freq[None, :]
    cr, sr, cc, sc = jnp.cos(ang_r), jnp.sin(ang_r), jnp.cos(ang_c), jnp.sin(ang_c)
    z = jnp.zeros_like(sr)
    c = jnp.concatenate([cr, cr, cc, cc], axis=1)
    s_up = jnp.concatenate([-sr, z, -sc, z], axis=1)
    s_dn = jnp.concatenate([z, sr, z, sc], axis=1)
    two = lambda t: jnp.concatenate([t, t], axis=1)
    return two(c), two(s_up), two(s_dn)


def _block_diag_ones(width):
    i = np.arange(width) // HEAD_DIM
    return jnp.asarray((i[:, None] == i[None, :]).astype(np.float32), dtype=BF16)


def _qkv_proj(x, g_attn, w_qkv, g_qa, g_ka, tm):
    b, s, _ = x.shape
    c, su, sd = _rope_tables(s)
    gq = jnp.tile(g_qa, Q_HEADS).reshape(1, Q_W)
    gk = jnp.tile(g_ka, KV_HEADS).reshape(1, KV_W)
    row = lambda w: pl.BlockSpec((1, tm, w), lambda i, j: (i, j, 0))
    tab = pl.BlockSpec((tm, KV_W), lambda i, j: (j, 0))
    const = lambda shp: pl.BlockSpec(shp, lambda i, j: (0,) * len(shp))
    sds = lambda w: jax.ShapeDtypeStruct((b, s, w), BF16)
    return pl.pallas_call(
        _qkv_kernel,
        grid=(b, s // tm),
        in_specs=[row(D_MODEL), const((1, D_MODEL)), const((D_MODEL, QKV_W)), const((1, Q_W)),
                  const((1, KV_W)), tab, tab, tab, const((Q_W, Q_W)), const((KV_W, KV_W))],
        out_specs=[row(Q_W), pl.BlockSpec((1, KV_W, tm), lambda i, j: (i, 0, j)), row(KV_W),
                   row(Q_W), row(KV_W), row(KV_W), row(M_W)],
        out_shape=[sds(Q_W), jax.ShapeDtypeStruct((b, KV_W, s), BF16), sds(KV_W),
                   sds(Q_W), sds(KV_W), sds(KV_W), sds(M_W)],
        compiler_params=_cparams(2),
        name="qkv_proj",
    )(x, g_attn.reshape(1, D_MODEL), w_qkv, gq, gk, c, su, sd,
      _block_diag_ones(Q_W), _block_diag_ones(KV_W))


def _kv_lane_mask(kvh, dtype):
    lane = lax.broadcasted_iota(I32, (1, KV_W), 1)
    return ((lane // HEAD_DIM) == kvh).astype(dtype)


def _attn_global_kernel(q_ref, kt_ref, v_ref, o_ref, m_sc, l_sc, acc_sc, *, seq, tk):
    tq = q_ref.shape[1]
    outs = []
    for head in range(Q_HEADS):
        kvh = head // GROUP
        qh = q_ref[0, :, head * HEAD_DIM:(head + 1) * HEAD_DIM]
        zeros = jnp.zeros_like(qh)
        qz = jnp.concatenate([qh, zeros] if kvh == 0 else [zeros, qh], axis=1)
        m_sc[...] = jnp.full((tq, 1), -jnp.inf, F32)
        l_sc[...] = jnp.zeros((tq, 1), F32)
        acc_sc[...] = jnp.zeros((tq, KV_W), F32)

        def body(ci, carry):
            off = pl.multiple_of(ci * tk, tk)
            s = _dot(qz, kt_ref[0, :, pl.ds(off, tk)])
            m_old = m_sc[...]
            m_new = jnp.maximum(m_old, jnp.max(s, axis=-1, keepdims=True))
            alpha = jnp.exp(m_old - m_new)
            p = jnp.exp(s - m_new)
            l_sc[...] = alpha * l_sc[...] + jnp.sum(p, axis=-1, keepdims=True)
            acc_sc[...] = alpha * acc_sc[...] + _dot(p.astype(BF16), v_ref[0, pl.ds(off, tk), :])
            m_sc[...] = m_new
            return carry

        lax.fori_loop(0, seq // tk, body, 0)
        o = acc_sc[...] / l_sc[...]
        outs.append(o[:, kvh * HEAD_DIM:(kvh + 1) * HEAD_DIM])
    o_ref[0] = jnp.concatenate(outs, axis=1).astype(BF16)


def _attn_global(qa, kat, va, tq, tk):
    b, s, _ = qa.shape
    return pl.pallas_call(
        functools.partial(_attn_global_kernel, seq=s, tk=tk),
        grid=(b, s // tq),
        in_specs=[
            pl.BlockSpec((1, tq, Q_W), lambda i, j: (i, j, 0)),
            pl.BlockSpec((1, KV_W, s), lambda i, j: (i, 0, 0)),
            pl.BlockSpec((1, s, KV_W), lambda i, j: (i, 0, 0)),
        ],
        out_specs=pl.BlockSpec((1, tq, Q_W), lambda i, j: (i, j, 0)),
        out_shape=jax.ShapeDtypeStruct((b, s, Q_W), BF16),
        scratch_shapes=[pltpu.VMEM((tq, 1), F32), pltpu.VMEM((tq, 1), F32),
                        pltpu.VMEM((tq, KV_W), F32)],
        compiler_params=_cparams(2),
        name="attn_global",
    )(qa, kat, va)


def _attn_window_kernel(slope_ref, sink_ref, q_ref, k_ref, v_ref, o_ref, *, seq):
    i = pl.program_id(1)
    start = pl.multiple_of(i * BLOCK, BLOCK)
    ks = k_ref[0, pl.ds(start, SPAN), :]
    vs = v_ref[0, pl.ds(start, SPAN), :]
    r = lax.broadcasted_iota(I32, (BLOCK, SPAN), 0)
    c = lax.broadcasted_iota(I32, (BLOCK, SPAN), 1)
    dist = r + WINDOW - c
    adist = jnp.abs(dist)
    key_pos = start - WINDOW + c
    valid = (adist <= WINDOW) & (key_pos >= 0) & (key_pos < seq)
    adist_f = adist.astype(F32)
    outs = []
    for head in range(Q_HEADS):
        kvh = head // GROUP
        qh = q_ref[0, :, head * HEAD_DIM:(head + 1) * HEAD_DIM]
        zeros = jnp.zeros_like(qh)
        qz = jnp.concatenate([qh, zeros] if kvh == 0 else [zeros, qh], axis=1)
        logits = _dot_nt(qz, ks) - slope_ref[head] * adist_f
        logits = jnp.where(valid, logits, NEG_INF)
        sink = sink_ref[head]
        m = jnp.maximum(jnp.max(logits, axis=-1, keepdims=True), sink)
        e = jnp.exp(logits - m)
        denom = jnp.sum(e, axis=-1, keepdims=True) + jnp.exp(sink - m)
        p = (e / denom).astype(BF16)
        o = _dot(p, vs)
        outs.append(o[:, kvh * HEAD_DIM:(kvh + 1) * HEAD_DIM])
    o_ref[0] = jnp.concatenate(outs, axis=1).astype(BF16)


def _attn_window(qb, kb, vb, slopes, sink):
    b, s, _ = qb.shape
    pad = ((0, 0), (WINDOW, WINDOW), (0, 0))
    kp, vp = jnp.pad(kb, pad), jnp.pad(vb, pad)
    smem = pl.BlockSpec(memory_space=pltpu.SMEM)
    return pl.pallas_call(
        functools.partial(_attn_window_kernel, seq=s),
        grid=(b, s // BLOCK),
        in_specs=[
            smem, smem,
            pl.BlockSpec((1, BLOCK, Q_W), lambda i, j: (i, j, 0)),
            pl.BlockSpec((1, s + 2 * WINDOW, KV_W), lambda i, j: (i, 0, 0)),
            pl.BlockSpec((1, s + 2 * WINDOW, KV_W), lambda i, j: (i, 0, 0)),
        ],
        out_specs=pl.BlockSpec((1, BLOCK, Q_W), lambda i, j: (i, j, 0)),
        out_shape=jax.ShapeDtypeStruct((b, s, Q_W), BF16),
        compiler_params=_cparams(2),
        name="attn_window",
    )(slopes, sink, qb, kp, vp)


def _attn_mem_kernel(q_ref, mkt_ref, mv_ref, o_ref):
    scale = M_HEAD_DIM ** -0.5
    outs = []
    for head in range(M_HEADS):
        sl = slice(head * M_HEAD_DIM, (head + 1) * M_HEAD_DIM)
        logits = _dot(q_ref[0, :, sl], mkt_ref[0, sl, :]) * scale
        m = jnp.max(logits, axis=-1, keepdims=True)
        e = jnp.exp(logits - m)
        p = (e / jnp.sum(e, axis=-1, keepdims=True)).astype(BF16)
        outs.append(_dot(p, mv_ref[0, :, sl]))
    o_ref[0] = jnp.concatenate(outs, axis=1).astype(BF16)


def _attn_mem(qm, mkt, mv, tq):
    b, s, _ = qm.shape
    return pl.pallas_call(
        _attn_mem_kernel,
        grid=(b, s // tq),
        in_specs=[
            pl.BlockSpec((1, tq, M_W), lambda i, j: (i, j, 0)),
            pl.BlockSpec((1, M_W, N_MEM), lambda i, j: (i, 0, 0)),
            pl.BlockSpec((1, N_MEM, M_W), lambda i, j: (i, 0, 0)),
        ],
        out_specs=pl.BlockSpec((1, tq, M_W), lambda i, j: (i, j, 0)),
        out_shape=jax.ShapeDtypeStruct((b, s, M_W), BF16),
        compiler_params=_cparams(2),
        name="attn_mem",
    )(qm, mkt, mv)


def _merge_kernel(x_ref, g_ref, wg_ref, oa_ref, ob_ref, om_ref, wpa_ref, wpb_ref, wpm_ref, wo_ref,
                  x2_ref):
    x = x_ref[...]
    h = _rms(x, g_ref[...]).astype(BF16)
    gates = jax.nn.sigmoid(_dot(h, wg_ref[...]))
    merged = (gates[:, :D_MODEL] * _dot(oa_ref[...], wpa_ref[...])
              + gates[:, D_MODEL:2 * D_MODEL] * _dot(ob_ref[...], wpb_ref[...])
              + gates[:, 2 * D_MODEL:] * _dot(om_ref[...], wpm_ref[...]))
    x2_ref[...] = x + _dot(merged.astype(BF16), wo_ref[...])


def _merge_proj(x, g_attn, w_gate, oa, ob, om, w_pa, w_pb, w_pm, w_o, tm):
    t = x.shape[0]
    row = lambda w: pl.BlockSpec((tm, w), lambda i: (i, 0))
    const = lambda shp: pl.BlockSpec(shp, lambda i: (0, 0))
    return pl.pallas_call(
        _merge_kernel,
        grid=(t // tm,),
        in_specs=[row(D_MODEL), const((1, D_MODEL)), const((D_MODEL, GATE_W)),
                  row(Q_W), row(Q_W), row(M_W),
                  const((Q_W, D_MODEL)), const((Q_W, D_MODEL)), const((M_W, D_MODEL)),
                  const((D_MODEL, D_MODEL))],
        out_specs=row(D_MODEL),
        out_shape=jax.ShapeDtypeStruct((t, D_MODEL), F32),
        compiler_params=_cparams(1),
        name="merge_proj",
    )(x, g_attn.reshape(1, D_MODEL), w_gate, oa, ob, om, w_pa, w_pb, w_pm, w_o)


def _topk_rows(s, aux, k):
    n = s.shape[0]
    iota = lax.broadcasted_iota(I32, s.shape, 0)
    vals, picks = [], []
    for _ in range(k):
        m = jnp.max(s, axis=0, keepdims=True)
        win = jnp.min(jnp.where(s == m, iota, n), axis=0, keepdims=True)
        sel = iota == win
        vals.append(m)
        if aux is None:
            picks.append(win)
        else:
            picks.append(jnp.max(jnp.where(sel, aux, -1), axis=0, keepdims=True))
        s = jnp.where(sel, -jnp.inf, s)
    return jnp.concatenate(vals, axis=0), jnp.concatenate(picks, axis=0)


def _route_kernel(x2_ref, g_ref, wq_ref, k1_ref, k2_ref, h2_ref, idx_ref, gate_ref):
    h2 = _rms(x2_ref[...], g_ref[...])
    h2_ref[...] = h2
    q = _dot(h2.astype(BF16), wq_ref[...]).astype(BF16)
    idx_rows, gate_rows = [], []
    for head in range(PEER_HEADS):
        o = head * 2 * PEER_HALF
        s1 = _dot_nt(k1_ref[head], q[:, o:o + PEER_HALF])
        s2 = _dot_nt(k2_ref[head], q[:, o + PEER_HALF:o + 2 * PEER_HALF])
        sv1, si1 = _topk_rows(s1, None, PEER_TOPK)
        sv2, si2 = _topk_rows(s2, None, PEER_TOPK)
        cand = jnp.concatenate([sv1[i:i + 1] + sv2 for i in range(PEER_TOPK)], axis=0)
        cidx = jnp.concatenate([si1[i:i + 1] * PEER_KEYS + si2 for i in range(PEER_TOPK)], axis=0)
        score, eidx = _topk_rows(cand, cidx, PEER_TOPK)
        e = jnp.exp(score - score[0:1])
        gate_rows.append(e / jnp.sum(e, axis=0, keepdims=True))
        idx_rows.append(eidx)
    idx_ref[...] = jnp.concatenate(idx_rows, axis=0).T
    gate_ref[...] = jnp.concatenate(gate_rows, axis=0).T


def _peer_route(x2, g_ffn, w_pq, keys1, keys2, tm):
    t = x2.shape[0]
    row = lambda w: pl.BlockSpec((tm, w), lambda i: (i, 0))
    const = lambda shp: pl.BlockSpec(shp, lambda i: (0,) * len(shp))
    kshape = (PEER_HEADS, PEER_KEYS, PEER_HALF)
    return pl.pallas_call(
        _route_kernel,
        grid=(t // tm,),
        in_specs=[row(D_MODEL), const((1, D_MODEL)), const((D_MODEL, PEER_HEADS * 2 * PEER_HALF)),
                  const(kshape), const(kshape)],
        out_specs=[row(D_MODEL), row(PEER_PAIRS), row(PEER_PAIRS)],
        out_shape=[jax.ShapeDtypeStruct((t, D_MODEL), F32),
                   jax.ShapeDtypeStruct((t, PEER_PAIRS), I32),
                   jax.ShapeDtypeStruct((t, PEER_PAIRS), F32)],
        compiler_params=_cparams(1),
        name="peer_route",
    )(x2, g_ffn.reshape(1, D_MODEL), w_pq, keys1, keys2)


def _pack_table(tab):
    e = tab.shape[0]
    bits = lax.bitcast_convert_type(tab.astype(BF16), jnp.uint16).astype(jnp.uint32)
    words = bits[:, :HALF_D] | (bits[:, HALF_D:] << 16)
    return lax.bitcast_convert_type(words, I32).reshape(e * PACK_ROWS, 128)


def _unpack_row(w):
    lo = lax.bitcast_convert_type(w << 16, F32)
    hi = lax.bitcast_convert_type(w & jnp.int32(-65536), F32)
    return lo, hi


def _table_spec(rows):
    return pl.BlockSpec((rows, 128), lambda i: (0, 0), pipeline_mode=pl.Buffered(1))


def _peer_act_kernel(idx_ref, h3_ref, gate_ref, tab_ref, coef_ref, part_sc, act_sc):
    tm = h3_ref.shape[0]

    def token(t, carry):
        x = h3_ref[t]
        xlo, xhi = x[0:PACK_ROWS], x[PACK_ROWS:]
        for k in range(PEER_PAIRS):
            e = idx_ref[t, k]
            w = tab_ref[pl.ds(pl.multiple_of(e * PACK_ROWS, PACK_ROWS), PACK_ROWS), :]
            lo, hi = _unpack_row(w)
            p = lo * xlo + hi * xhi
            p = p[0:2] + p[2:4]
            part_sc[k:k + 1, :] = p[0:1] + p[1:2]
        act_sc[pl.ds(t, 1), :] = jnp.sum(part_sc[...].T, axis=0, keepdims=True)
        return carry

    lax.fori_loop(0, tm, token, 0)
    a = act_sc[...]
    gelu = 0.5 * a * (1.0 + lax.erf(a * (2.0 ** -0.5)))
    coef_ref[...] = gelu * gate_ref[...]


def _peer_act(idx, h3, gate, utab, tm):
    t = idx.shape[0]
    return pl.pallas_call(
        _peer_act_kernel,
        grid=(t // tm,),
        in_specs=[
            pl.BlockSpec((tm, PEER_PAIRS), lambda i: (i, 0), memory_space=pltpu.SMEM),
            pl.BlockSpec((tm, 8, 128), lambda i: (i, 0, 0)),
            pl.BlockSpec((tm, PEER_PAIRS), lambda i: (i, 0)),
            _table_spec(utab.shape[0]),
        ],
        out_specs=pl.BlockSpec((tm, PEER_PAIRS), lambda i: (i, 0)),
        out_shape=jax.ShapeDtypeStruct((t, PEER_PAIRS), F32),
        scratch_shapes=[pltpu.VMEM((PEER_PAIRS, 128), F32), pltpu.VMEM((tm, PEER_PAIRS), F32)],
        compiler_params=_cparams(1),
        name="peer_act",
    )(idx, h3, gate, utab)


def _peer_out_kernel(idx_ref, coef_ref, x3_ref, g_ref, tab_ref, y_ref, o_sc):
    tm = x3_ref.shape[0]
    n_acc = 2

    def token(t, carry):
        acc_lo = [jnp.zeros((PACK_ROWS, 128), F32) for _ in range(n_acc)]
        acc_hi = [jnp.zeros((PACK_ROWS, 128), F32) for _ in range(n_acc)]
        for k in range(PEER_PAIRS):
            e = idx_ref[t, k]
            c = coef_ref[t, k]
            w = tab_ref[pl.ds(pl.multiple_of(e * PACK_ROWS, PACK_ROWS), PACK_ROWS), :]
            lo, hi = _unpack_row(w)
            acc_lo[k % n_acc] = acc_lo[k % n_acc] + c * lo
            acc_hi[k % n_acc] = acc_hi[k % n_acc] + c * hi
        o_sc[t] = jnp.concatenate([acc_lo[0] + acc_lo[1], acc_hi[0] + acc_hi[1]], axis=0)
        return carry

    lax.fori_loop(0, tm, token, 0)
    z = x3_ref[...] + o_sc[...]
    ss = jnp.sum(jnp.sum(z * z, axis=2, keepdims=True), axis=1, keepdims=True)
    y_ref[...] = (z * lax.rsqrt(ss * (1.0 / D_MODEL) + EPS)) * g_ref[...]


def _peer_out(idx, coef, x3, g_final, vtab, tm):
    t = idx.shape[0]
    smem_row = pl.BlockSpec((tm, PEER_PAIRS), lambda i: (i, 0), memory_space=pltpu.SMEM)
    return pl.pallas_call(
        _peer_out_kernel,
        grid=(t // tm,),
        in_specs=[
            smem_row, smem_row,
            pl.BlockSpec((tm, 8, 128), lambda i: (i, 0, 0)),
            pl.BlockSpec((1, 8, 128), lambda i: (0, 0, 0)),
            _table_spec(vtab.shape[0]),
        ],
        out_specs=pl.BlockSpec((tm, 8, 128), lambda i: (i, 0, 0)),
        out_shape=jax.ShapeDtypeStruct((t, 8, 128), F32),
        scratch_shapes=[pltpu.VMEM((tm, 8, 128), F32)],
        compiler_params=_cparams(1),
        name="peer_out",
    )(idx, coef, x3, g_final.reshape(1, 8, 128), vtab)


def _pick(n, prefs):
    for p in prefs:
        if n % p == 0:
            return p
    raise ValueError(f"no tile in {prefs} divides {n}")


def _trunk(x, mem, w, g_final):
    b, s, _ = x.shape
    t = b * s
    mkt, mv = _mem_kv(mem, w["g_mem"], w["w_mkv"])
    qa, kat, va, qb, kb, vb, qm = _qkv_proj(x, w["g_attn"], w["w_qkv"], w["g_qa"], w["g_ka"],
                                           _pick(s, (512, 256, 128)))
    oa = _attn_global(qa, kat, va, _pick(s, (256, 128)), _pick(s, (512, 256, 128)))
    ob = _attn_window(qb, kb, vb, w["slopes"], w["sink_b"])
    om = _attn_mem(qm, mkt, mv, _pick(s, (512, 256, 128)))
    flat = lambda a: a.reshape(t, a.shape[-1])
    x2 = _merge_proj(flat(x), w["g_attn"], w["w_gate"], flat(oa), flat(ob), flat(om),
                     w["w_pa"], w["w_pb"], w["w_pm"], w["w_o"], _pick(t, (256, 128)))
    h2, idx, gate = _peer_route(x2, w["g_ffn"], w["w_pq"], w["keys1"], w["keys2"],
                                _pick(t, (256, 128)))
    tm = _pick(t, (64, 32, 16, 8))
    coef = _peer_act(idx, h2.reshape(t, 8, 128), gate, w["utab"], tm)
    y3 = _peer_out(idx, coef, x2.reshape(t, 8, 128), g_final, w["vtab"], tm)
    return y3.reshape(b, s, D_MODEL)


def kernel(x_prompt, x_sample, mem_prompt, mem_sample, g_attn, w_in, g_qa, g_ka, sink_b, w_mkv,
           g_mem, w_pa, w_pb, w_pm, w_o, g_ffn, w_pq, peer_keys1, peer_keys2, peer_u, peer_v,
           g_final):
    assert g_attn.shape[0] == 1, "single-layer trunk"
    w_in_b = w_in[0].astype(BF16)
    w = dict(
        g_attn=g_attn[0], w_qkv=w_in_b[:, :QKV_W], w_gate=w_in_b[:, QKV_W:],
        g_qa=g_qa[0], g_ka=g_ka[0], sink_b=sink_b[0], w_mkv=w_mkv[0], g_mem=g_mem[0],
        w_pa=w_pa[0].astype(BF16), w_pb=w_pb[0].astype(BF16), w_pm=w_pm[0].astype(BF16),
        w_o=w_o[0].astype(BF16), g_ffn=g_ffn[0], w_pq=w_pq[0].astype(BF16),
        keys1=peer_keys1[0].astype(BF16), keys2=peer_keys2[0].astype(BF16),
        utab=_pack_table(peer_u[0]), vtab=_pack_table(peer_v[0]),
        slopes=2.0 ** (-8.0 * jnp.arange(1, Q_HEADS + 1, dtype=F32) / Q_HEADS),
    )
    return (_trunk(x_prompt, mem_prompt, w, g_final), _trunk(x_sample, mem_sample, w, g_final))
```

```python
import functools

import jax
import jax.numpy as jnp
import numpy as np
from jax import lax
from jax.experimental import pallas as pl
from jax.experimental.pallas import tpu as pltpu

F32 = jnp.float32
BF16 = jnp.bfloat16
I32 = jnp.int32

D_MODEL = 1024
HEAD_DIM = 64
Q_HEADS = 8
KV_HEADS = 2
GROUP = Q_HEADS // KV_HEADS
Q_W = Q_HEADS * HEAD_DIM
KV_W = KV_HEADS * HEAD_DIM
M_HEADS = 4
M_HEAD_DIM = 128
M_W = M_HEADS * M_HEAD_DIM
N_MEM = 256
QKV_W = 2 * (Q_W + 2 * KV_W) + M_W
GATE_W = 3 * D_MODEL
GRID_W = 64
WINDOW = 128
BLOCK = 128
SPAN = BLOCK + 2 * WINDOW
ROPE_THETA = 10000.0
ROPE_HALF = HEAD_DIM // 4
PEER_HEADS = 8
PEER_KEYS = 128
PEER_HALF = 128
PEER_TOPK = 16
PEER_PAIRS = PEER_HEADS * PEER_TOPK
EPS = 1e-6
NEG_INF = -1e30

VMEM_LIMIT_BYTES = 56 * 1024 * 1024
PACK_ROWS = 4
HALF_D = D_MODEL // 2


def _cparams(n_axes):
    return pltpu.CompilerParams(
        dimension_semantics=("arbitrary",) * n_axes,
        vmem_limit_bytes=VMEM_LIMIT_BYTES,
    )


def _rms(x, g):
    ms = jnp.mean(x * x, axis=-1, keepdims=True)
    return (x * lax.rsqrt(ms + EPS)) * g


def _dot(a, b):
    return jnp.dot(a, b, preferred_element_type=F32)


def _dot_nt(a, b):
    return lax.dot_general(a, b, (((1,), (1,)), ((), ())), preferred_element_type=F32)


def _dot_f32_by_01(a, m01):
    hi = a.astype(BF16)
    r1 = a - hi.astype(F32)
    mid = r1.astype(BF16)
    lo = (r1 - mid.astype(F32)).astype(BF16)
    return _dot(lo, m01) + _dot(mid, m01) + _dot(hi, m01)


def _mem_kv_kernel(mem_ref, g_ref, w_ref, mkt_ref, mv_ref):
    h = _rms(mem_ref[0], g_ref[...]).astype(BF16)
    z = _dot(h, w_ref[...])
    mkt_ref[0] = z[:, :M_W].T.astype(BF16)
    mv_ref[0] = z[:, M_W:].astype(BF16)


def _mem_kv(mem, g_mem, w_mkv):
    b = mem.shape[0]
    return pl.pallas_call(
        _mem_kv_kernel,
        grid=(b,),
        in_specs=[
            pl.BlockSpec((1, N_MEM, D_MODEL), lambda i: (i, 0, 0)),
            pl.BlockSpec((1, D_MODEL), lambda i: (0, 0)),
            pl.BlockSpec((D_MODEL, 2 * M_W), lambda i: (0, 0)),
        ],
        out_specs=[
            pl.BlockSpec((1, M_W, N_MEM), lambda i: (i, 0, 0)),
            pl.BlockSpec((1, N_MEM, M_W), lambda i: (i, 0, 0)),
        ],
        out_shape=[
            jax.ShapeDtypeStruct((b, M_W, N_MEM), BF16),
            jax.ShapeDtypeStruct((b, N_MEM, M_W), BF16),
        ],
        compiler_params=_cparams(1),
        name="mem_kv",
    )(mem, g_mem.reshape(1, D_MODEL), w_mkv.astype(BF16))


def _head_norm_rope(q, g, bd, c, s_up, s_dn, width):
    ss = _dot_f32_by_01(q * q, bd)
    qn = (q * lax.rsqrt(ss * (1.0 / HEAD_DIM) + EPS)) * g
    up = pltpu.roll(qn, width - ROPE_HALF, 1)
    dn = pltpu.roll(qn, ROPE_HALF, 1)
    return qn * c + up * s_up + dn * s_dn


def _qkv_kernel(x_ref, g_ref, w_ref, gq_ref, gk_ref, c_ref, su_ref, sd_ref, bdq_ref, bdk_ref,
                qa_ref, kat_ref, va_ref, qb_ref, kb_ref, vb_ref, qm_ref):
    h = _rms(x_ref[0], g_ref[...]).astype(BF16)
    z = _dot(h, w_ref[...])
    c2, su2, sd2 = c_ref[...], su_ref[...], sd_ref[...]
    rep = Q_W // KV_W
    c8 = jnp.concatenate([c2] * rep, axis=1)
    su8 = jnp.concatenate([su2] * rep, axis=1)
    sd8 = jnp.concatenate([sd2] * rep, axis=1)
    o = 0
    qa = _head_norm_rope(z[:, o:o + Q_W], gq_ref[...], bdq_ref[...], c8, su8, sd8, Q_W)
    qa_ref[0] = (qa * (HEAD_DIM ** -0.5)).astype(BF16)
    o += Q_W
    ka = _head_norm_rope(z[:, o:o + KV_W], gk_ref[...], bdk_ref[...], c2, su2, sd2, KV_W)
    kat_ref[0] = ka.T.astype(BF16)
    o += KV_W
    va_ref[0] = z[:, o:o + KV_W].astype(BF16)
    o += KV_W
    qb_ref[0] = (z[:, o:o + Q_W] * (HEAD_DIM ** -0.5)).astype(BF16)
    o += Q_W
    kb_ref[0] = z[:, o:o + KV_W].astype(BF16)
    o += KV_W
    vb_ref[0] = z[:, o:o + KV_W].astype(BF16)
    o += KV_W
    qm_ref[0] = z[:, o:o + M_W].astype(BF16)


def _rope_tables(seq_len):
    rows = seq_len // GRID_W
    row_ids = jnp.repeat(jnp.arange(rows, dtype=F32), GRID_W)
    col_ids = jnp.tile(jnp.arange(GRID_W, dtype=F32), rows)
    inv_freq = ROPE_THETA ** (-jnp.arange(ROPE_HALF, dtype=F32) / ROPE_HALF)
    ang_r = row_ids[:, None] * inv_freq[None, :]
    ang_c = col_ids[:, None] * inv_freq[None, :]
    cr, sr, cc, sc = jnp.cos(ang_r), jnp.sin(ang_r), jnp.cos(ang_c), jnp.sin(ang_c)
    z = jnp.zeros_like(sr)
    c = jnp.concatenate([cr, cr, cc, cc], axis=1)
    s_up = jnp.concatenate([-sr, z, -sc, z], axis=1)
    s_dn = jnp.concatenate([z, sr, z, sc], axis=1)
    two = lambda t: jnp.concatenate([t, t], axis=1)
    return two(c), two(s_up), two(s_dn)


def _block_diag_ones(width):
    i = np.arange(width) // HEAD_DIM
    return jnp.asarray((i[:, None] == i[None, :]).astype(np.float32), dtype=BF16)


def _qkv_proj(x, g_attn, w_qkv, g_qa, g_ka, tm):
    b, s, _ = x.shape
    c, su, sd = _rope_tables(s)
    gq = jnp.tile(g_qa, Q_HEADS).reshape(1, Q_W)
    gk = jnp.tile(g_ka, KV_HEADS).reshape(1, KV_W)
    row = lambda w: pl.BlockSpec((1, tm, w), lambda i, j: (i, j, 0))
    tab = pl.BlockSpec((tm, KV_W), lambda i, j: (j, 0))
    const = lambda shp: pl.BlockSpec(shp, lambda i, j: (0,) * len(shp))
    sds = lambda w: jax.ShapeDtypeStruct((b, s, w), BF16)
    return pl.pallas_call(
        _qkv_kernel,
        grid=(b, s // tm),
        in_specs=[row(D_MODEL), const((1, D_MODEL)), const((D_MODEL, QKV_W)), const((1, Q_W)),
                  const((1, KV_W)), tab, tab, tab, const((Q_W, Q_W)), const((KV_W, KV_W))],
        out_specs=[row(Q_W), pl.BlockSpec((1, KV_W, tm), lambda i, j: (i, 0, j)), row(KV_W),
                   row(Q_W), row(KV_W), row(KV_W), row(M_W)],
        out_shape=[sds(Q_W), jax.ShapeDtypeStruct((b, KV_W, s), BF16), sds(KV_W),
                   sds(Q_W), sds(KV_W), sds(KV_W), sds(M_W)],
        compiler_params=_cparams(2),
        name="qkv_proj",
    )(x, g_attn.reshape(1, D_MODEL), w_qkv, gq, gk, c, su, sd,
      _block_diag_ones(Q_W), _block_diag_ones(KV_W))


def _kv_lane_mask(kvh, dtype):
    lane = lax.broadcasted_iota(I32, (1, KV_W), 1)
    return ((lane // HEAD_DIM) == kvh).astype(dtype)


def _attn_global_kernel(q_ref, kt_ref, v_ref, o_ref, q4_sc, m_sc, l_sc, acc_sc, *, seq, tk):
    tq = q_ref.shape[1]
    rows = GROUP * tq
    n_slab = tk // 128
    outs = [None] * Q_HEADS
    for kvh in range(KV_HEADS):
        for gi in range(GROUP):
            head = kvh * GROUP + gi
            qh = q_ref[0, :, head * HEAD_DIM:(head + 1) * HEAD_DIM]
            zeros = jnp.zeros_like(qh)
            q4_sc[gi * tq:(gi + 1) * tq, :] = jnp.concatenate(
                [qh, zeros] if kvh == 0 else [zeros, qh], axis=1)
        m_sc[...] = jnp.full((rows, 128), -jnp.inf, F32)

        def row_max(ci, carry):
            off = pl.multiple_of(ci * tk, tk)
            s = _dot(q4_sc[...], kt_ref[0, :, pl.ds(off, tk)])
            m = m_sc[...]
            for j in range(n_slab):
                m = jnp.maximum(m, s[:, j * 128:(j + 1) * 128])
            m_sc[...] = m
            return carry

        lax.fori_loop(0, seq // tk, row_max, 0)
        m_sc[...] = jnp.broadcast_to(jnp.max(m_sc[...], axis=-1, keepdims=True), (rows, 128))
        l_sc[...] = jnp.zeros((rows, 128), F32)
        acc_sc[...] = jnp.zeros((rows, KV_W), F32)

        def accumulate(ci, carry):
            off = pl.multiple_of(ci * tk, tk)
            s = _dot(q4_sc[...], kt_ref[0, :, pl.ds(off, tk)])
            m = m_sc[...]
            l = l_sc[...]
            ps = []
            for j in range(n_slab):
                pj = jnp.exp(s[:, j * 128:(j + 1) * 128] - m)
                l = l + pj
                ps.append(pj.astype(BF16))
            l_sc[...] = l
            acc_sc[...] += _dot(jnp.concatenate(ps, axis=1), v_ref[0, pl.ds(off, tk), :])
            return carry

        lax.fori_loop(0, seq // tk, accumulate, 0)
        o = acc_sc[...] / jnp.sum(l_sc[...], axis=-1, keepdims=True)
        for gi in range(GROUP):
            outs[kvh * GROUP + gi] = o[gi * tq:(gi + 1) * tq, kvh * HEAD_DIM:(kvh + 1) * HEAD_DIM]
    o_ref[0] = jnp.concatenate(outs, axis=1).astype(BF16)


def _attn_global(qa, kat, va, tq, tk):
    b, s, _ = qa.shape
    return pl.pallas_call(
        functools.partial(_attn_global_kernel, seq=s, tk=tk),
        grid=(b, s // tq),
        in_specs=[
            pl.BlockSpec((1, tq, Q_W), lambda i, j: (i, j, 0)),
            pl.BlockSpec((1, KV_W, s), lambda i, j: (i, 0, 0)),
            pl.BlockSpec((1, s, KV_W), lambda i, j: (i, 0, 0)),
        ],
        out_specs=pl.BlockSpec((1, tq, Q_W), lambda i, j: (i, j, 0)),
        out_shape=jax.ShapeDtypeStruct((b, s, Q_W), BF16),
        scratch_shapes=[pltpu.VMEM((GROUP * tq, KV_W), BF16), pltpu.VMEM((GROUP * tq, 128), F32),
                        pltpu.VMEM((GROUP * tq, 128), F32), pltpu.VMEM((GROUP * tq, KV_W), F32)],
        compiler_params=_cparams(2),
        name="attn_global",
    )(qa, kat, va)


def _attn_window_kernel(slope_ref, sink_ref, q_ref, k_ref, v_ref, o_ref, *, seq):
    i = pl.program_id(1)
    start = pl.multiple_of(i * BLOCK, BLOCK)
    ks = k_ref[0, pl.ds(start, SPAN), :]
    vs = v_ref[0, pl.ds(start, SPAN), :]
    r = lax.broadcasted_iota(I32, (BLOCK, SPAN), 0)
    c = lax.broadcasted_iota(I32, (BLOCK, SPAN), 1)
    dist = r + WINDOW - c
    adist = jnp.abs(dist)
    key_pos = start - WINDOW + c
    valid = (adist <= WINDOW) & (key_pos >= 0) & (key_pos < seq)
    adist_f = adist.astype(F32)
    outs = []
    for head in range(Q_HEADS):
        kvh = head // GROUP
        qh = q_ref[0, :, head * HEAD_DIM:(head + 1) * HEAD_DIM]
        zeros = jnp.zeros_like(qh)
        qz = jnp.concatenate([qh, zeros] if kvh == 0 else [zeros, qh], axis=1)
        logits = _dot_nt(qz, ks) - slope_ref[head] * adist_f
        logits = jnp.where(valid, logits, NEG_INF)
        sink = sink_ref[head]
        m = jnp.maximum(jnp.max(logits, axis=-1, keepdims=True), sink)
        e = jnp.exp(logits - m)
        denom = jnp.sum(e, axis=-1, keepdims=True) + jnp.exp(sink - m)
        p = (e / denom).astype(BF16)
        o = _dot(p, vs)
        outs.append(o[:, kvh * HEAD_DIM:(kvh + 1) * HEAD_DIM])
    o_ref[0] = jnp.concatenate(outs, axis=1).astype(BF16)


def _attn_window(qb, kb, vb, slopes, sink):
    b, s, _ = qb.shape
    pad = ((0, 0), (WINDOW, WINDOW), (0, 0))
    kp, vp = jnp.pad(kb, pad), jnp.pad(vb, pad)
    smem = pl.BlockSpec(memory_space=pltpu.SMEM)
    return pl.pallas_call(
        functools.partial(_attn_window_kernel, seq=s),
        grid=(b, s // BLOCK),
        in_specs=[
            smem, smem,
            pl.BlockSpec((1, BLOCK, Q_W), lambda i, j: (i, j, 0)),
            pl.BlockSpec((1, s + 2 * WINDOW, KV_W), lambda i, j: (i, 0, 0)),
            pl.BlockSpec((1, s + 2 * WINDOW, KV_W), lambda i, j: (i, 0, 0)),
        ],
        out_specs=pl.BlockSpec((1, BLOCK, Q_W), lambda i, j: (i, j, 0)),
        out_shape=jax.ShapeDtypeStruct((b, s, Q_W), BF16),
        compiler_params=_cparams(2),
        name="attn_window",
    )(slopes, sink, qb, kp, vp)


def _attn_mem_kernel(q_ref, mkt_ref, mv_ref, o_ref):
    scale = M_HEAD_DIM ** -0.5
    outs = []
    for head in range(M_HEADS):
        sl = slice(head * M_HEAD_DIM, (head + 1) * M_HEAD_DIM)
        logits = _dot(q_ref[0, :, sl], mkt_ref[0, sl, :]) * scale
        m = jnp.max(logits, axis=-1, keepdims=True)
        e = jnp.exp(logits - m)
        p = (e / jnp.sum(e, axis=-1, keepdims=True)).astype(BF16)
        outs.append(_dot(p, mv_ref[0, :, sl]))
    o_ref[0] = jnp.concatenate(outs, axis=1).astype(BF16)


def _attn_mem(qm, mkt, mv, tq):
    b, s, _ = qm.shape
    return pl.pallas_call(
        _attn_mem_kernel,
        grid=(b, s // tq),
        in_specs=[
            pl.BlockSpec((1, tq, M_W), lambda i, j: (i, j, 0)),
            pl.BlockSpec((1, M_W, N_MEM), lambda i, j: (i, 0, 0)),
            pl.BlockSpec((1, N_MEM, M_W), lambda i, j: (i, 0, 0)),
        ],
        out_specs=pl.BlockSpec((1, tq, M_W), lambda i, j: (i, j, 0)),
        out_shape=jax.ShapeDtypeStruct((b, s, M_W), BF16),
        compiler_params=_cparams(2),
        name="attn_mem",
    )(qm, mkt, mv)


def _merge_kernel(x_ref, g_ref, wg_ref, oa_ref, ob_ref, om_ref, wpa_ref, wpb_ref, wpm_ref, wo_ref,
                  x2_ref):
    x = x_ref[...]
    h = _rms(x, g_ref[...]).astype(BF16)
    gates = jax.nn.sigmoid(_dot(h, wg_ref[...]))
    merged = (gates[:, :D_MODEL] * _dot(oa_ref[...], wpa_ref[...])
              + gates[:, D_MODEL:2 * D_MODEL] * _dot(ob_ref[...], wpb_ref[...])
              + gates[:, 2 * D_MODEL:] * _dot(om_ref[...], wpm_ref[...]))
    x2_ref[...] = x + _dot(merged.astype(BF16), wo_ref[...])


def _merge_proj(x, g_attn, w_gate, oa, ob, om, w_pa, w_pb, w_pm, w_o, tm):
    t = x.shape[0]
    row = lambda w: pl.BlockSpec((tm, w), lambda i: (i, 0))
    const = lambda shp: pl.BlockSpec(shp, lambda i: (0, 0))
    return pl.pallas_call(
        _merge_kernel,
        grid=(t // tm,),
        in_specs=[row(D_MODEL), const((1, D_MODEL)), const((D_MODEL, GATE_W)),
                  row(Q_W), row(Q_W), row(M_W),
                  const((Q_W, D_MODEL)), const((Q_W, D_MODEL)), const((M_W, D_MODEL)),
                  const((D_MODEL, D_MODEL))],
        out_specs=row(D_MODEL),
        out_shape=jax.ShapeDtypeStruct((t, D_MODEL), F32),
        compiler_params=_cparams(1),
        name="merge_proj",
    )(x, g_attn.reshape(1, D_MODEL), w_gate, oa, ob, om, w_pa, w_pb, w_pm, w_o)


def _topk_rows(s, aux, k):
    n = s.shape[0]
    iota = lax.broadcasted_iota(I32, s.shape, 0).astype(F32)
    vals, picks = [], []
    for _ in range(k):
        m = jnp.max(s, axis=0, keepdims=True)
        win = jnp.min(jnp.where(s == m, iota, float(n)), axis=0, keepdims=True)
        sel = iota == win
        vals.append(m)
        if aux is None:
            picks.append(win)
        else:
            picks.append(jnp.max(jnp.where(sel, aux, -1.0), axis=0, keepdims=True))
        s = jnp.where(sel, -jnp.inf, s)
    return jnp.concatenate(vals, axis=0), jnp.concatenate(picks, axis=0)


def _pair_candidates(a, b, combine):
    half = PEER_TOPK // 2
    rows = [combine(a[0:1], b)]
    rows += [combine(a[i:i + 1], b[0:half]) for i in range(1, half)]
    rows.append(combine(a[half:], b[0:1]))
    return jnp.concatenate(rows, axis=0)


def _route_kernel(x2_ref, g_ref, wq_ref, k1_ref, k2_ref, h2_ref, idx_ref, gate_ref):
    h2 = _rms(x2_ref[...], g_ref[...])
    h2_ref[...] = h2
    q = _dot(h2.astype(BF16), wq_ref[...]).astype(BF16)
    idx_rows, gate_rows = [], []
    for head in range(PEER_HEADS):
        o = head * 2 * PEER_HALF
        s1 = _dot_nt(k1_ref[head], q[:, o:o + PEER_HALF])
        s2 = _dot_nt(k2_ref[head], q[:, o + PEER_HALF:o + 2 * PEER_HALF])
        sv1, si1 = _topk_rows(s1, None, PEER_TOPK)
        sv2, si2 = _topk_rows(s2, None, PEER_TOPK)
        cand = _pair_candidates(sv1, sv2, lambda a, b: a + b)
        cidx = _pair_candidates(si1, si2, lambda a, b: a * float(PEER_KEYS) + b)
        score, eidx = _topk_rows(cand, cidx, PEER_TOPK)
        e = jnp.exp(score - score[0:1])
        gate_rows.append(e / jnp.sum(e, axis=0, keepdims=True))
        idx_rows.append(eidx)
    idx = (jnp.concatenate(idx_rows, axis=0) * float(PACK_ROWS)).astype(I32)
    idx_ref[...] = idx.T
    gate_ref[...] = jnp.concatenate(gate_rows, axis=0).T


def _peer_route(x2, g_ffn, w_pq, keys1, keys2, tm):
    t = x2.shape[0]
    row = lambda w: pl.BlockSpec((tm, w), lambda i: (i, 0))
    const = lambda shp: pl.BlockSpec(shp, lambda i: (0,) * len(shp))
    kshape = (PEER_HEADS, PEER_KEYS, PEER_HALF)
    return pl.pallas_call(
        _route_kernel,
        grid=(t // tm,),
        in_specs=[row(D_MODEL), const((1, D_MODEL)), const((D_MODEL, PEER_HEADS * 2 * PEER_HALF)),
                  const(kshape), const(kshape)],
        out_specs=[row(D_MODEL), row(PEER_PAIRS), row(PEER_PAIRS)],
        out_shape=[jax.ShapeDtypeStruct((t, D_MODEL), F32),
                   jax.ShapeDtypeStruct((t, PEER_PAIRS), I32),
                   jax.ShapeDtypeStruct((t, PEER_PAIRS), F32)],
        compiler_params=_cparams(1),
        name="peer_route",
    )(x2, g_ffn.reshape(1, D_MODEL), w_pq, keys1, keys2)


def _pack_table(tab):
    e = tab.shape[0]
    bits = lax.bitcast_convert_type(tab.astype(BF16), jnp.uint16).astype(jnp.uint32)
    words = bits[:, :HALF_D] | (bits[:, HALF_D:] << 16)
    return lax.bitcast_convert_type(words, I32).reshape(e * PACK_ROWS, 128)


def _unpack_row(w):
    lo = lax.bitcast_convert_type(w << 16, F32)
    hi = lax.bitcast_convert_type(w & jnp.int32(-65536), F32)
    return lo, hi


def _table_spec(rows):
    return pl.BlockSpec((rows, 128), lambda i: (0, 0), pipeline_mode=pl.Buffered(1))


def _peer_act_kernel(idx_ref, h3_ref, gate_ref, tab_ref, coef_ref, part_a, part_b, act_sc):
    tm = h3_ref.shape[0]

    def products(t, part_sc):
        x = h3_ref[t]
        xlo, xhi = x[0:PACK_ROWS], x[PACK_ROWS:]
        for k in range(PEER_PAIRS):
            row = pl.multiple_of(idx_ref[t, k], PACK_ROWS)
            lo, hi = _unpack_row(tab_ref[pl.ds(row, PACK_ROWS), :])
            part_sc[k * PACK_ROWS:(k + 1) * PACK_ROWS, :] = lo * xlo + hi * xhi

    def reduce(t, part_sc):
        part = part_sc[pl.ds(0, PEER_PAIRS, stride=PACK_ROWS), :]
        for s in range(1, PACK_ROWS):
            part = part + part_sc[pl.ds(s, PEER_PAIRS, stride=PACK_ROWS), :]
        act_sc[pl.ds(t, 1), :] = jnp.sum(part.T, axis=0, keepdims=True)

    part_b[...] = jnp.zeros(part_b.shape, F32)

    def token_pair(j, carry):
        t0 = 2 * j
        products(t0, part_a)
        reduce(jnp.maximum(t0 - 1, 0), part_b)
        products(t0 + 1, part_b)
        reduce(t0, part_a)
        return carry

    lax.fori_loop(0, tm // 2, token_pair, 0)
    reduce(tm - 1, part_b)
    a = act_sc[...]
    gelu = 0.5 * a * (1.0 + lax.erf(a * (2.0 ** -0.5)))
    coef_ref[...] = gelu * gate_ref[...]


def _peer_act(idx, h3, gate, utab, tm):
    t = idx.shape[0]
    return pl.pallas_call(
        _peer_act_kernel,
        grid=(t // tm,),
        in_specs=[
            pl.BlockSpec((tm, PEER_PAIRS), lambda i: (i, 0), memory_space=pltpu.SMEM),
            pl.BlockSpec((tm, 8, 128), lambda i: (i, 0, 0)),
            pl.BlockSpec((tm, PEER_PAIRS), lambda i: (i, 0)),
            _table_spec(utab.shape[0]),
        ],
        out_specs=pl.BlockSpec((tm, PEER_PAIRS), lambda i: (i, 0)),
        out_shape=jax.ShapeDtypeStruct((t, PEER_PAIRS), F32),
        scratch_shapes=[pltpu.VMEM((PEER_PAIRS * PACK_ROWS, 128), F32),
                        pltpu.VMEM((PEER_PAIRS * PACK_ROWS, 128), F32),
                        pltpu.VMEM((tm, PEER_PAIRS), F32)],
        compiler_params=_cparams(1),
        name="peer_act",
    )(idx, h3, gate, utab)


def _peer_out_kernel(idx_ref, coef_ref, x3_ref, g_ref, tab_ref, y_ref, o_sc, cb_a, cb_b):
    tm = x3_ref.shape[0]
    n_acc = 4

    def spread(t, cb_sc):
        cb_sc[...] = jnp.broadcast_to(coef_ref[pl.ds(t, 1), :], (PEER_PAIRS, PEER_PAIRS)).T

    def combine(t, cb_sc):
        acc_lo = [jnp.zeros((PACK_ROWS, 128), F32) for _ in range(n_acc)]
        acc_hi = [jnp.zeros((PACK_ROWS, 128), F32) for _ in range(n_acc)]
        for k in range(PEER_PAIRS):
            row = pl.multiple_of(idx_ref[t, k], PACK_ROWS)
            lo, hi = _unpack_row(tab_ref[pl.ds(row, PACK_ROWS), :])
            c = jnp.broadcast_to(cb_sc[k:k + 1, :], (PACK_ROWS, 128))
            acc_lo[k % n_acc] = acc_lo[k % n_acc] + c * lo
            acc_hi[k % n_acc] = acc_hi[k % n_acc] + c * hi
        lo = (acc_lo[0] + acc_lo[1]) + (acc_lo[2] + acc_lo[3])
        hi = (acc_hi[0] + acc_hi[1]) + (acc_hi[2] + acc_hi[3])
        o_sc[t] = jnp.concatenate([lo, hi], axis=0)

    spread(0, cb_a)

    def token_pair(j, carry):
        t0 = 2 * j
        spread(t0 + 1, cb_b)
        combine(t0, cb_a)
        spread(jnp.minimum(t0 + 2, tm - 1), cb_a)
        combine(t0 + 1, cb_b)
        return carry

    lax.fori_loop(0, tm // 2, token_pair, 0)
    z = x3_ref[...] + o_sc[...]
    ss = jnp.sum(jnp.sum(z * z, axis=2, keepdims=True), axis=1, keepdims=True)
    y_ref[...] = (z * lax.rsqrt(ss * (1.0 / D_MODEL) + EPS)) * g_ref[...]


def _peer_out(idx, coef, x3, g_final, vtab, tm):
    t = idx.shape[0]
    return pl.pallas_call(
        _peer_out_kernel,
        grid=(t // tm,),
        in_specs=[
            pl.BlockSpec((tm, PEER_PAIRS), lambda i: (i, 0), memory_space=pltpu.SMEM),
            pl.BlockSpec((tm, PEER_PAIRS), lambda i: (i, 0)),
            pl.BlockSpec((tm, 8, 128), lambda i: (i, 0, 0)),
            pl.BlockSpec((1, 8, 128), lambda i: (0, 0, 0)),
            _table_spec(vtab.shape[0]),
        ],
        out_specs=pl.BlockSpec((tm, 8, 128), lambda i: (i, 0, 0)),
        out_shape=jax.ShapeDtypeStruct((t, 8, 128), F32),
        scratch_shapes=[pltpu.VMEM((tm, 8, 128), F32), pltpu.VMEM((PEER_PAIRS, PEER_PAIRS), F32),
                        pltpu.VMEM((PEER_PAIRS, PEER_PAIRS), F32)],
        compiler_params=_cparams(1),
        name="peer_out",
    )(idx, coef, x3, g_final.reshape(1, 8, 128), vtab)


def _pick(n, prefs):
    for p in prefs:
        if n % p == 0:
            return p
    raise ValueError(f"no tile in {prefs} divides {n}")


def _trunk(x, mem, w, g_final):
    b, s, _ = x.shape
    t = b * s
    mkt, mv = _mem_kv(mem, w["g_mem"], w["w_mkv"])
    qa, kat, va, qb, kb, vb, qm = _qkv_proj(x, w["g_attn"], w["w_qkv"], w["g_qa"], w["g_ka"],
                                           _pick(s, (512, 256, 128)))
    oa = _attn_global(qa, kat, va, _pick(s, (256, 128)), _pick(s, (512, 256, 128)))
    ob = _attn_window(qb, kb, vb, w["slopes"], w["sink_b"])
    om = _attn_mem(qm, mkt, mv, _pick(s, (512, 256, 128)))
    flat = lambda a: a.reshape(t, a.shape[-1])
    x2 = _merge_proj(flat(x), w["g_attn"], w["w_gate"], flat(oa), flat(ob), flat(om),
                     w["w_pa"], w["w_pb"], w["w_pm"], w["w_o"], _pick(t, (256, 128)))
    h2, idx, gate = _peer_route(x2, w["g_ffn"], w["w_pq"], w["keys1"], w["keys2"],
                                _pick(t, (256, 128)))
    tm = _pick(t, (64, 32, 16, 8))
    coef = _peer_act(idx, h2.reshape(t, 8, 128), gate, w["utab"], tm)
    y3 = _peer_out(idx, coef, x2.reshape(t, 8, 128), g_final, w["vtab"], tm)
    return y3.reshape(b, s, D_MODEL)


def kernel(x_prompt, x_sample, mem_prompt, mem_sample, g_attn, w_in, g_qa, g_ka, sink_b, w_mkv,
           g_mem, w_pa, w_pb, w_pm, w_o, g_ffn, w_pq, peer_keys1, peer_keys2, peer_u, peer_v,
           g_final):
    assert g_attn.shape[0] == 1, "single-layer trunk"
    w_in_b = w_in[0].astype(BF16)
    w = dict(
        g_attn=g_attn[0], w_qkv=w_in_b[:, :QKV_W], w_gate=w_in_b[:, QKV_W:],
        g_qa=g_qa[0], g_ka=g_ka[0], sink_b=sink_b[0], w_mkv=w_mkv[0], g_mem=g_mem[0],
        w_pa=w_pa[0].astype(BF16), w_pb=w_pb[0].astype(BF16), w_pm=w_pm[0].astype(BF16),
        w_o=w_o[0].astype(BF16), g_ffn=g_ffn[0], w_pq=w_pq[0].astype(BF16),
        keys1=peer_keys1[0].astype(BF16), keys2=peer_keys2[0].astype(BF16),
        utab=_pack_table(peer_u[0]), vtab=_pack_table(peer_v[0]),
        slopes=2.0 ** (-8.0 * jnp.arange(1, Q_HEADS + 1, dtype=F32) / Q_HEADS),
    )
    return (_trunk(x_prompt, mem_prompt, w, g_final), _trunk(x_sample, mem_sample, w, g_final))
```

```python
import functools

import jax
import jax.numpy as jnp
import numpy as np
from jax import lax
from jax.experimental import pallas as pl
from jax.experimental.pallas import tpu as pltpu

F32 = jnp.float32
BF16 = jnp.bfloat16
I32 = jnp.int32

D_MODEL = 1024
HEAD_DIM = 64
Q_HEADS = 8
KV_HEADS = 2
GROUP = Q_HEADS // KV_HEADS
Q_W = Q_HEADS * HEAD_DIM
KV_W = KV_HEADS * HEAD_DIM
M_HEADS = 4
M_HEAD_DIM = 128
M_W = M_HEADS * M_HEAD_DIM
N_MEM = 256
QKV_W = 2 * (Q_W + 2 * KV_W) + M_W
GATE_W = 3 * D_MODEL
GRID_W = 64
WINDOW = 128
BLOCK = 128
SPAN = BLOCK + 2 * WINDOW
ROPE_THETA = 10000.0
ROPE_HALF = HEAD_DIM // 4
PEER_HEADS = 8
PEER_KEYS = 128
PEER_HALF = 128
PEER_TOPK = 16
PEER_PAIRS = PEER_HEADS * PEER_TOPK
EPS = 1e-6
NEG_INF = -1e30

VMEM_LIMIT_BYTES = 56 * 1024 * 1024
PACK_ROWS = 4
HALF_D = D_MODEL // 2


def _cparams(n_axes):
    return pltpu.CompilerParams(
        dimension_semantics=("arbitrary",) * n_axes,
        vmem_limit_bytes=VMEM_LIMIT_BYTES,
    )


def _rms(x, g):
    ms = jnp.mean(x * x, axis=-1, keepdims=True)
    return (x * lax.rsqrt(ms + EPS)) * g


def _dot(a, b):
    return jnp.dot(a, b, preferred_element_type=F32)


def _dot_nt(a, b):
    return lax.dot_general(a, b, (((1,), (1,)), ((), ())), preferred_element_type=F32)


def _dot_f32_by_01(a, m01):
    hi = a.astype(BF16)
    r1 = a - hi.astype(F32)
    mid = r1.astype(BF16)
    lo = (r1 - mid.astype(F32)).astype(BF16)
    return _dot(lo, m01) + _dot(mid, m01) + _dot(hi, m01)


def _mem_kv_kernel(mem_ref, g_ref, w_ref, mkt_ref, mv_ref):
    h = _rms(mem_ref[0], g_ref[...]).astype(BF16)
    z = _dot(h, w_ref[...])
    mkt_ref[0] = z[:, :M_W].T.astype(BF16)
    mv_ref[0] = z[:, M_W:].astype(BF16)


def _mem_kv(mem, g_mem, w_mkv):
    b = mem.shape[0]
    return pl.pallas_call(
        _mem_kv_kernel,
        grid=(b,),
        in_specs=[
            pl.BlockSpec((1, N_MEM, D_MODEL), lambda i: (i, 0, 0)),
            pl.BlockSpec((1, D_MODEL), lambda i: (0, 0)),
            pl.BlockSpec((D_MODEL, 2 * M_W), lambda i: (0, 0)),
        ],
        out_specs=[
            pl.BlockSpec((1, M_W, N_MEM), lambda i: (i, 0, 0)),
            pl.BlockSpec((1, N_MEM, M_W), lambda i: (i, 0, 0)),
        ],
        out_shape=[
            jax.ShapeDtypeStruct((b, M_W, N_MEM), BF16),
            jax.ShapeDtypeStruct((b, N_MEM, M_W), BF16),
        ],
        compiler_params=_cparams(1),
        name="mem_kv",
    )(mem, g_mem.reshape(1, D_MODEL), w_mkv.astype(BF16))


def _head_norm_rope(q, g, bd, c, s_up, s_dn, width):
    ss = _dot_f32_by_01(q * q, bd)
    qn = (q * lax.rsqrt(ss * (1.0 / HEAD_DIM) + EPS)) * g
    up = pltpu.roll(qn, width - ROPE_HALF, 1)
    dn = pltpu.roll(qn, ROPE_HALF, 1)
    return qn * c + up * s_up + dn * s_dn


def _qkv_kernel(x_ref, g_ref, w_ref, gq_ref, gk_ref, c_ref, su_ref, sd_ref, bdq_ref, bdk_ref,
                qa_ref, kat_ref, va_ref, qb_ref, kb_ref, vb_ref, qm_ref):
    h = _rms(x_ref[0], g_ref[...]).astype(BF16)
    z = _dot(h, w_ref[...])
    c2, su2, sd2 = c_ref[...], su_ref[...], sd_ref[...]
    rep = Q_W // KV_W
    c8 = jnp.concatenate([c2] * rep, axis=1)
    su8 = jnp.concatenate([su2] * rep, axis=1)
    sd8 = jnp.concatenate([sd2] * rep, axis=1)
    o = 0
    qa = _head_norm_rope(z[:, o:o + Q_W], gq_ref[...], bdq_ref[...], c8, su8, sd8, Q_W)
    qa_ref[0] = (qa * (HEAD_DIM ** -0.5)).astype(BF16)
    o += Q_W
    ka = _head_norm_rope(z[:, o:o + KV_W], gk_ref[...], bdk_ref[...], c2, su2, sd2, KV_W)
    kat_ref[0] = ka.T.astype(BF16)
    o += KV_W
    va_ref[0] = z[:, o:o + KV_W].astype(BF16)
    o += KV_W
    qb_ref[0] = (z[:, o:o + Q_W] * (HEAD_DIM ** -0.5)).astype(BF16)
    o += Q_W
    kb_ref[0] = z[:, o:o + KV_W].astype(BF16)
    o += KV_W
    vb_ref[0] = z[:, o:o + KV_W].astype(BF16)
    o += KV_W
    qm_ref[0] = z[:, o:o + M_W].astype(BF16)


def _rope_tables(seq_len):
    rows = seq_len // GRID_W
    row_ids = jnp.repeat(jnp.arange(rows, dtype=F32), GRID_W)
    col_ids = jnp.tile(jnp.arange(GRID_W, dtype=F32), rows)
    inv_freq = ROPE_THETA ** (-jnp.arange(ROPE_HALF, dtype=F32) / ROPE_HALF)
    ang_r = row_ids[:, None] * inv_freq[None, :]
    ang_c = col_ids[:, None] * inv_freq[None, :]
    cr, sr, cc, sc = jnp.cos(ang_r), jnp.sin(ang_r), jnp.cos(ang_c), jnp.sin(ang_c)
    z = jnp.zeros_like(sr)
    c = jnp.concatenate([cr, cr, cc, cc], axis=1)
    s_up = jnp.concatenate([-sr, z, -sc, z], axis=1)
    s_dn = jnp.concatenate([z, sr, z, sc], axis=1)
    two = lambda t: jnp.concatenate([t, t], axis=1)
    return two(c), two(s_up), two(s_dn)


def _block_diag_ones(width):
    i = np.arange(width) // HEAD_DIM
    return jnp.asarray((i[:, None] == i[None, :]).astype(np.float32), dtype=BF16)


def _qkv_proj(x, g_attn, w_qkv, g_qa, g_ka, tm):
    b, s, _ = x.shape
    c, su, sd = _rope_tables(s)
    gq = jnp.tile(g_qa, Q_HEADS).reshape(1, Q_W)
    gk = jnp.tile(g_ka, KV_HEADS).reshape(1, KV_W)
    row = lambda w: pl.BlockSpec((1, tm, w), lambda i, j: (i, j, 0))
    tab = pl.BlockSpec((tm, KV_W), lambda i, j: (j, 0))
    const = lambda shp: pl.BlockSpec(shp, lambda i, j: (0,) * len(shp))
    sds = lambda w: jax.ShapeDtypeStruct((b, s, w), BF16)
    return pl.pallas_call(
        _qkv_kernel,
        grid=(b, s // tm),
        in_specs=[row(D_MODEL), const((1, D_MODEL)), const((D_MODEL, QKV_W)), const((1, Q_W)),
                  const((1, KV_W)), tab, tab, tab, const((Q_W, Q_W)), const((KV_W, KV_W))],
        out_specs=[row(Q_W), pl.BlockSpec((1, KV_W, tm), lambda i, j: (i, 0, j)), row(KV_W),
                   row(Q_W), row(KV_W), row(KV_W), row(M_W)],
        out_shape=[sds(Q_W), jax.ShapeDtypeStruct((b, KV_W, s), BF16), sds(KV_W),
                   sds(Q_W), sds(KV_W), sds(KV_W), sds(M_W)],
        compiler_params=_cparams(2),
        name="qkv_proj",
    )(x, g_attn.reshape(1, D_MODEL), w_qkv, gq, gk, c, su, sd,
      _block_diag_ones(Q_W), _block_diag_ones(KV_W))


def _kv_lane_mask(kvh, dtype):
    lane = lax.broadcasted_iota(I32, (1, KV_W), 1)
    return ((lane // HEAD_DIM) == kvh).astype(dtype)


def _attn_global_kernel(q_ref, kt_ref, v_ref, o_ref, q4_sc, m_sc, l_sc, acc_sc, *, seq, tk):
    tq = q_ref.shape[1]
    rows = GROUP * tq
    n_slab = tk // 128
    outs = [None] * Q_HEADS
    for kvh in range(KV_HEADS):
        for gi in range(GROUP):
            head = kvh * GROUP + gi
            qh = q_ref[0, :, head * HEAD_DIM:(head + 1) * HEAD_DIM]
            zeros = jnp.zeros_like(qh)
            q4_sc[gi * tq:(gi + 1) * tq, :] = jnp.concatenate(
                [qh, zeros] if kvh == 0 else [zeros, qh], axis=1)
        m_sc[...] = jnp.full((rows, 128), -jnp.inf, F32)

        def row_max(ci, carry):
            off = pl.multiple_of(ci * tk, tk)
            s = _dot(q4_sc[...], kt_ref[0, :, pl.ds(off, tk)])
            m = m_sc[...]
            for j in range(n_slab):
                m = jnp.maximum(m, s[:, j * 128:(j + 1) * 128])
            m_sc[...] = m
            return carry

        lax.fori_loop(0, seq // tk, row_max, 0)
        m_sc[...] = jnp.broadcast_to(jnp.max(m_sc[...], axis=-1, keepdims=True), (rows, 128))
        l_sc[...] = jnp.zeros((rows, 128), F32)
        acc_sc[...] = jnp.zeros((rows, KV_W), F32)

        def accumulate(ci, carry):
            off = pl.multiple_of(ci * tk, tk)
            s = _dot(q4_sc[...], kt_ref[0, :, pl.ds(off, tk)])
            m = m_sc[...]
            l = l_sc[...]
            ps = []
            for j in range(n_slab):
                pj = jnp.exp(s[:, j * 128:(j + 1) * 128] - m)
                l = l + pj
                ps.append(pj.astype(BF16))
            l_sc[...] = l
            acc_sc[...] += _dot(jnp.concatenate(ps, axis=1), v_ref[0, pl.ds(off, tk), :])
            return carry

        lax.fori_loop(0, seq // tk, accumulate, 0)
        o = acc_sc[...] / jnp.sum(l_sc[...], axis=-1, keepdims=True)
        for gi in range(GROUP):
            outs[kvh * GROUP + gi] = o[gi * tq:(gi + 1) * tq, kvh * HEAD_DIM:(kvh + 1) * HEAD_DIM]
    o_ref[0] = jnp.concatenate(outs, axis=1).astype(BF16)


def _attn_global(qa, kat, va, tq, tk):
    b, s, _ = qa.shape
    return pl.pallas_call(
        functools.partial(_attn_global_kernel, seq=s, tk=tk),
        grid=(b, s // tq),
        in_specs=[
            pl.BlockSpec((1, tq, Q_W), lambda i, j: (i, j, 0)),
            pl.BlockSpec((1, KV_W, s), lambda i, j: (i, 0, 0)),
            pl.BlockSpec((1, s, KV_W), lambda i, j: (i, 0, 0)),
        ],
        out_specs=pl.BlockSpec((1, tq, Q_W), lambda i, j: (i, j, 0)),
        out_shape=jax.ShapeDtypeStruct((b, s, Q_W), BF16),
        scratch_shapes=[pltpu.VMEM((GROUP * tq, KV_W), BF16), pltpu.VMEM((GROUP * tq, 128), F32),
                        pltpu.VMEM((GROUP * tq, 128), F32), pltpu.VMEM((GROUP * tq, KV_W), F32)],
        compiler_params=_cparams(2),
        name="attn_global",
    )(qa, kat, va)


def _attn_window_kernel(slope_ref, sink_ref, q_ref, k_ref, v_ref, o_ref, *, seq):
    i = pl.program_id(1)
    start = pl.multiple_of(i * BLOCK, BLOCK)
    ks = k_ref[0, pl.ds(start, SPAN), :]
    vs = v_ref[0, pl.ds(start, SPAN), :]
    r = lax.broadcasted_iota(I32, (BLOCK, SPAN), 0)
    c = lax.broadcasted_iota(I32, (BLOCK, SPAN), 1)
    dist = r + WINDOW - c
    adist = jnp.abs(dist)
    key_pos = start - WINDOW + c
    valid = (adist <= WINDOW) & (key_pos >= 0) & (key_pos < seq)
    adist_f = adist.astype(F32)
    outs = []
    for head in range(Q_HEADS):
        kvh = head // GROUP
        qh = q_ref[0, :, head * HEAD_DIM:(head + 1) * HEAD_DIM]
        zeros = jnp.zeros_like(qh)
        qz = jnp.concatenate([qh, zeros] if kvh == 0 else [zeros, qh], axis=1)
        logits = _dot_nt(qz, ks) - slope_ref[head] * adist_f
        logits = jnp.where(valid, logits, NEG_INF)
        sink = sink_ref[head]
        m = jnp.maximum(jnp.max(logits, axis=-1, keepdims=True), sink)
        e = jnp.exp(logits - m)
        denom = jnp.sum(e, axis=-1, keepdims=True) + jnp.exp(sink - m)
        p = (e / denom).astype(BF16)
        o = _dot(p, vs)
        outs.append(o[:, kvh * HEAD_DIM:(kvh + 1) * HEAD_DIM])
    o_ref[0] = jnp.concatenate(outs, axis=1).astype(BF16)


def _attn_window(qb, kb, vb, slopes, sink):
    b, s, _ = qb.shape
    pad = ((0, 0), (WINDOW, WINDOW), (0, 0))
    kp, vp = jnp.pad(kb, pad), jnp.pad(vb, pad)
    smem = pl.BlockSpec(memory_space=pltpu.SMEM)
    return pl.pallas_call(
        functools.partial(_attn_window_kernel, seq=s),
        grid=(b, s // BLOCK),
        in_specs=[
            smem, smem,
            pl.BlockSpec((1, BLOCK, Q_W), lambda i, j: (i, j, 0)),
            pl.BlockSpec((1, s + 2 * WINDOW, KV_W), lambda i, j: (i, 0, 0)),
            pl.BlockSpec((1, s + 2 * WINDOW, KV_W), lambda i, j: (i, 0, 0)),
        ],
        out_specs=pl.BlockSpec((1, BLOCK, Q_W), lambda i, j: (i, j, 0)),
        out_shape=jax.ShapeDtypeStruct((b, s, Q_W), BF16),
        compiler_params=_cparams(2),
        name="attn_window",
    )(slopes, sink, qb, kp, vp)


def _attn_mem_kernel(q_ref, mkt_ref, mv_ref, o_ref):
    scale = M_HEAD_DIM ** -0.5
    outs = []
    for head in range(M_HEADS):
        sl = slice(head * M_HEAD_DIM, (head + 1) * M_HEAD_DIM)
        logits = _dot(q_ref[0, :, sl], mkt_ref[0, sl, :]) * scale
        m = jnp.max(logits, axis=-1, keepdims=True)
        e = jnp.exp(logits - m)
        p = (e / jnp.sum(e, axis=-1, keepdims=True)).astype(BF16)
        outs.append(_dot(p, mv_ref[0, :, sl]))
    o_ref[0] = jnp.concatenate(outs, axis=1).astype(BF16)


def _attn_mem(qm, mkt, mv, tq):
    b, s, _ = qm.shape
    return pl.pallas_call(
        _attn_mem_kernel,
        grid=(b, s // tq),
        in_specs=[
            pl.BlockSpec((1, tq, M_W), lambda i, j: (i, j, 0)),
            pl.BlockSpec((1, M_W, N_MEM), lambda i, j: (i, 0, 0)),
            pl.BlockSpec((1, N_MEM, M_W), lambda i, j: (i, 0, 0)),
        ],
        out_specs=pl.BlockSpec((1, tq, M_W), lambda i, j: (i, j, 0)),
        out_shape=jax.ShapeDtypeStruct((b, s, M_W), BF16),
        compiler_params=_cparams(2),
        name="attn_mem",
    )(qm, mkt, mv)


def _merge_kernel(x_ref, g_ref, wg_ref, oa_ref, ob_ref, om_ref, wpa_ref, wpb_ref, wpm_ref, wo_ref,
                  x2_ref):
    x = x_ref[...]
    h = _rms(x, g_ref[...]).astype(BF16)
    gates = jax.nn.sigmoid(_dot(h, wg_ref[...]))
    merged = (gates[:, :D_MODEL] * _dot(oa_ref[...], wpa_ref[...])
              + gates[:, D_MODEL:2 * D_MODEL] * _dot(ob_ref[...], wpb_ref[...])
              + gates[:, 2 * D_MODEL:] * _dot(om_ref[...], wpm_ref[...]))
    x2_ref[...] = x + _dot(merged.astype(BF16), wo_ref[...])


def _merge_proj(x, g_attn, w_gate, oa, ob, om, w_pa, w_pb, w_pm, w_o, tm):
    t = x.shape[0]
    row = lambda w: pl.BlockSpec((tm, w), lambda i: (i, 0))
    const = lambda shp: pl.BlockSpec(shp, lambda i: (0, 0))
    return pl.pallas_call(
        _merge_kernel,
        grid=(t // tm,),
        in_specs=[row(D_MODEL), const((1, D_MODEL)), const((D_MODEL, GATE_W)),
                  row(Q_W), row(Q_W), row(M_W),
                  const((Q_W, D_MODEL)), const((Q_W, D_MODEL)), const((M_W, D_MODEL)),
                  const((D_MODEL, D_MODEL))],
        out_specs=row(D_MODEL),
        out_shape=jax.ShapeDtypeStruct((t, D_MODEL), F32),
        compiler_params=_cparams(1),
        name="merge_proj",
    )(x, g_attn.reshape(1, D_MODEL), w_gate, oa, ob, om, w_pa, w_pb, w_pm, w_o)


def _topk_rows(s, aux, k):
    n = s.shape[0]
    iota = lax.broadcasted_iota(I32, s.shape, 0).astype(F32)
    vals, picks = [], []
    for _ in range(k):
        m = jnp.max(s, axis=0, keepdims=True)
        win = jnp.min(jnp.where(s == m, iota, float(n)), axis=0, keepdims=True)
        sel = iota == win
        vals.append(m)
        if aux is None:
            picks.append(win)
        else:
            picks.append(jnp.max(jnp.where(sel, aux, -1.0), axis=0, keepdims=True))
        s = jnp.where(sel, -jnp.inf, s)
    return jnp.concatenate(vals, axis=0), jnp.concatenate(picks, axis=0)


def _pair_candidates(a, b, combine):
    half = PEER_TOPK // 2
    rows = [combine(a[0:1], b)]
    rows += [combine(a[i:i + 1], b[0:half]) for i in range(1, half)]
    rows.append(combine(a[half:], b[0:1]))
    return jnp.concatenate(rows, axis=0)


def _route_kernel(x2_ref, g_ref, wq_ref, k1_ref, k2_ref, h2_ref, idx_ref, gate_ref):
    h2 = _rms(x2_ref[...], g_ref[...])
    h2_ref[...] = h2
    q = _dot(h2.astype(BF16), wq_ref[...]).astype(BF16)
    idx_rows, gate_rows = [], []
    for head in range(PEER_HEADS):
        o = head * 2 * PEER_HALF
        s1 = _dot_nt(k1_ref[head], q[:, o:o + PEER_HALF])
        s2 = _dot_nt(k2_ref[head], q[:, o + PEER_HALF:o + 2 * PEER_HALF])
        sv1, si1 = _topk_rows(s1, None, PEER_TOPK)
        sv2, si2 = _topk_rows(s2, None, PEER_TOPK)
        cand = _pair_candidates(sv1, sv2, lambda a, b: a + b)
        cidx = _pair_candidates(si1, si2, lambda a, b: a * float(PEER_KEYS) + b)
        score, eidx = _topk_rows(cand, cidx, PEER_TOPK)
        e = jnp.exp(score - score[0:1])
        gate_rows.append(e / jnp.sum(e, axis=0, keepdims=True))
        idx_rows.append(eidx)
    idx = (jnp.concatenate(idx_rows, axis=0) * float(PACK_ROWS)).astype(I32)
    idx_ref[...] = idx.T
    gate_ref[...] = jnp.concatenate(gate_rows, axis=0).T


def _peer_route(x2, g_ffn, w_pq, keys1, keys2, tm):
    t = x2.shape[0]
    row = lambda w: pl.BlockSpec((tm, w), lambda i: (i, 0))
    const = lambda shp: pl.BlockSpec(shp, lambda i: (0,) * len(shp))
    kshape = (PEER_HEADS, PEER_KEYS, PEER_HALF)
    return pl.pallas_call(
        _route_kernel,
        grid=(t // tm,),
        in_specs=[row(D_MODEL), const((1, D_MODEL)), const((D_MODEL, PEER_HEADS * 2 * PEER_HALF)),
                  const(kshape), const(kshape)],
        out_specs=[row(D_MODEL), row(PEER_PAIRS), row(PEER_PAIRS)],
        out_shape=[jax.ShapeDtypeStruct((t, D_MODEL), F32),
                   jax.ShapeDtypeStruct((t, PEER_PAIRS), I32),
                   jax.ShapeDtypeStruct((t, PEER_PAIRS), F32)],
        compiler_params=_cparams(1),
        name="peer_route",
    )(x2, g_ffn.reshape(1, D_MODEL), w_pq, keys1, keys2)


def _pack_table(tab):
    e = tab.shape[0]
    bits = lax.bitcast_convert_type(tab.astype(BF16), jnp.uint16).astype(jnp.uint32)
    words = bits[:, :HALF_D] | (bits[:, HALF_D:] << 16)
    return lax.bitcast_convert_type(words, I32).reshape(e * PACK_ROWS, 128)


def _unpack_row(w):
    lo = lax.bitcast_convert_type(w << 16, F32)
    hi = lax.bitcast_convert_type(w & jnp.int32(-65536), F32)
    return lo, hi


def _table_spec(rows):
    return pl.BlockSpec((rows, 128), lambda i: (0, 0), pipeline_mode=pl.Buffered(1))


ID_GROUP = 8
ID_SLOT = ID_GROUP * PEER_PAIRS


def _for_id_groups(idx_hbm, idx_sm, sem, tm, body):
    n_groups = tm // ID_GROUP
    base = pl.program_id(0) * (tm * PEER_PAIRS)

    def copy(g, slot):
        src = idx_hbm.at[pl.ds(base + g * ID_SLOT, ID_SLOT)]
        return pltpu.make_async_copy(src, idx_sm[slot], sem.at[slot])

    copy(0, 0).start()

    def group_pair(j, carry):
        g0 = 2 * j
        copy(g0 + 1, 1).start()
        copy(g0, 0).wait()
        body(g0 * ID_GROUP, idx_sm[0])
        copy(jnp.minimum(g0 + 2, n_groups - 1), 0).start()
        copy(g0 + 1, 1).wait()
        body((g0 + 1) * ID_GROUP, idx_sm[1])
        return carry

    lax.fori_loop(0, n_groups // 2, group_pair, 0)
    copy(n_groups - 1, 0).wait()


def _id_scratch():
    return [pltpu.SMEM((ID_SLOT,), I32), pltpu.SMEM((ID_SLOT,), I32), pltpu.SemaphoreType.DMA((2,))]


def _peer_act_kernel(idx_hbm, h3_ref, gate_ref, tab_ref, coef_ref, part_a, part_b, act_sc,
                     ids_a, ids_b, sem):
    tm = h3_ref.shape[0]

    def group(t0, ids):
        for s in range(ID_GROUP):
            part_sc = part_b if s % 2 else part_a
            x = h3_ref[t0 + s]
            xlo, xhi = x[0:PACK_ROWS], x[PACK_ROWS:]
            for k in range(PEER_PAIRS):
                row = pl.multiple_of(ids[s * PEER_PAIRS + k], PACK_ROWS)
                lo, hi = _unpack_row(tab_ref[pl.ds(row, PACK_ROWS), :])
                part_sc[k * PACK_ROWS:(k + 1) * PACK_ROWS, :] = lo * xlo + hi * xhi
            part = part_sc[pl.ds(0, PEER_PAIRS, stride=PACK_ROWS), :]
            for r in range(1, PACK_ROWS):
                part = part + part_sc[pl.ds(r, PEER_PAIRS, stride=PACK_ROWS), :]
            act_sc[pl.ds(t0 + s, 1), :] = jnp.sum(part.T, axis=0, keepdims=True)

    _for_id_groups(idx_hbm, (ids_a, ids_b), sem, tm, group)
    a = act_sc[...]
    gelu = 0.5 * a * (1.0 + lax.erf(a * (2.0 ** -0.5)))
    coef_ref[...] = gelu * gate_ref[...]


def _peer_act(idx, h3, gate, utab, tm):
    t = h3.shape[0]
    return pl.pallas_call(
        _peer_act_kernel,
        grid=(t // tm,),
        in_specs=[
            pl.BlockSpec(memory_space=pl.ANY),
            pl.BlockSpec((tm, 8, 128), lambda i: (i, 0, 0)),
            pl.BlockSpec((tm, PEER_PAIRS), lambda i: (i, 0)),
            _table_spec(utab.shape[0]),
        ],
        out_specs=pl.BlockSpec((tm, PEER_PAIRS), lambda i: (i, 0)),
        out_shape=jax.ShapeDtypeStruct((t, PEER_PAIRS), F32),
        scratch_shapes=[pltpu.VMEM((PEER_PAIRS * PACK_ROWS, 128), F32),
                        pltpu.VMEM((PEER_PAIRS * PACK_ROWS, 128), F32),
                        pltpu.VMEM((tm, PEER_PAIRS), F32)] + _id_scratch(),
        compiler_params=_cparams(1),
        name="peer_act",
    )(idx.reshape(-1), h3, gate, utab)


def _peer_out_kernel(idx_hbm, coef_ref, x3_ref, g_ref, tab_ref, y_ref, o_sc, cb_a, cb_b,
                     ids_a, ids_b, sem):
    tm = x3_ref.shape[0]
    n_acc = 4

    def spread(t, cb_sc):
        cb_sc[...] = jnp.broadcast_to(coef_ref[pl.ds(t, 1), :], (PEER_PAIRS, PEER_PAIRS)).T

    def group(t0, ids):
        spread(t0, cb_a)
        for s in range(ID_GROUP):
            cb_sc = cb_b if s % 2 else cb_a
            if s + 1 < ID_GROUP:
                spread(t0 + s + 1, cb_a if s % 2 else cb_b)
            acc_lo = [jnp.zeros((PACK_ROWS, 128), F32) for _ in range(n_acc)]
            acc_hi = [jnp.zeros((PACK_ROWS, 128), F32) for _ in range(n_acc)]
            for k in range(PEER_PAIRS):
                row = pl.multiple_of(ids[s * PEER_PAIRS + k], PACK_ROWS)
                lo, hi = _unpack_row(tab_ref[pl.ds(row, PACK_ROWS), :])
                c = jnp.broadcast_to(cb_sc[k:k + 1, :], (PACK_ROWS, 128))
                acc_lo[k % n_acc] = acc_lo[k % n_acc] + c * lo
                acc_hi[k % n_acc] = acc_hi[k % n_acc] + c * hi
            lo = (acc_lo[0] + acc_lo[1]) + (acc_lo[2] + acc_lo[3])
            hi = (acc_hi[0] + acc_hi[1]) + (acc_hi[2] + acc_hi[3])
            o_sc[t0 + s] = jnp.concatenate([lo, hi], axis=0)

    _for_id_groups(idx_hbm, (ids_a, ids_b), sem, tm, group)
    z = x3_ref[...] + o_sc[...]
    ss = jnp.sum(jnp.sum(z * z, axis=2, keepdims=True), axis=1, keepdims=True)
    y_ref[...] = (z * lax.rsqrt(ss * (1.0 / D_MODEL) + EPS)) * g_ref[...]


def _peer_out(idx, coef, x3, g_final, vtab, tm):
    t = x3.shape[0]
    return pl.pallas_call(
        _peer_out_kernel,
        grid=(t // tm,),
        in_specs=[
            pl.BlockSpec(memory_space=pl.ANY),
            pl.BlockSpec((tm, PEER_PAIRS), lambda i: (i, 0)),
            pl.BlockSpec((tm, 8, 128), lambda i: (i, 0, 0)),
            pl.BlockSpec((1, 8, 128), lambda i: (0, 0, 0)),
            _table_spec(vtab.shape[0]),
        ],
        out_specs=pl.BlockSpec((tm, 8, 128), lambda i: (i, 0, 0)),
        out_shape=jax.ShapeDtypeStruct((t, 8, 128), F32),
        scratch_shapes=[pltpu.VMEM((tm, 8, 128), F32), pltpu.VMEM((PEER_PAIRS, PEER_PAIRS), F32),
                        pltpu.VMEM((PEER_PAIRS, PEER_PAIRS), F32)] + _id_scratch(),
        compiler_params=_cparams(1),
        name="peer_out",
    )(idx.reshape(-1), coef, x3, g_final.reshape(1, 8, 128), vtab)


def _pick(n, prefs):
    for p in prefs:
        if n % p == 0:
            return p
    raise ValueError(f"no tile in {prefs} divides {n}")


def _trunk(x, mem, w, g_final):
    b, s, _ = x.shape
    t = b * s
    mkt, mv = _mem_kv(mem, w["g_mem"], w["w_mkv"])
    qa, kat, va, qb, kb, vb, qm = _qkv_proj(x, w["g_attn"], w["w_qkv"], w["g_qa"], w["g_ka"],
                                           _pick(s, (512, 256, 128)))
    oa = _attn_global(qa, kat, va, _pick(s, (256, 128)), _pick(s, (512, 256, 128)))
    ob = _attn_window(qb, kb, vb, w["slopes"], w["sink_b"])
    om = _attn_mem(qm, mkt, mv, _pick(s, (512, 256, 128)))
    flat = lambda a: a.reshape(t, a.shape[-1])
    x2 = _merge_proj(flat(x), w["g_attn"], w["w_gate"], flat(oa), flat(ob), flat(om),
                     w["w_pa"], w["w_pb"], w["w_pm"], w["w_o"], _pick(t, (256, 128)))
    h2, idx, gate = _peer_route(x2, w["g_ffn"], w["w_pq"], w["keys1"], w["keys2"],
                                _pick(t, (256, 128)))
    tm = _pick(t, (64, 32, 16, 8))
    coef = _peer_act(idx, h2.reshape(t, 8, 128), gate, w["utab"], tm)
    y3 = _peer_out(idx, coef, x2.reshape(t, 8, 128), g_final, w["vtab"], tm)
    return y3.reshape(b, s, D_MODEL)


def kernel(x_prompt, x_sample, mem_prompt, mem_sample, g_attn, w_in, g_qa, g_ka, sink_b, w_mkv,
           g_mem, w_pa, w_pb, w_pm, w_o, g_ffn, w_pq, peer_keys1, peer_keys2, peer_u, peer_v,
           g_final):
    assert g_attn.shape[0] == 1, "single-layer trunk"
    w_in_b = w_in[0].astype(BF16)
    w = dict(
        g_attn=g_attn[0], w_qkv=w_in_b[:, :QKV_W], w_gate=w_in_b[:, QKV_W:],
        g_qa=g_qa[0], g_ka=g_ka[0], sink_b=sink_b[0], w_mkv=w_mkv[0], g_mem=g_mem[0],
        w_pa=w_pa[0].astype(BF16), w_pb=w_pb[0].astype(BF16), w_pm=w_pm[0].astype(BF16),
        w_o=w_o[0].astype(BF16), g_ffn=g_ffn[0], w_pq=w_pq[0].astype(BF16),
        keys1=peer_keys1[0].astype(BF16), keys2=peer_keys2[0].astype(BF16),
        utab=_pack_table(peer_u[0]), vtab=_pack_table(peer_v[0]),
        slopes=2.0 ** (-8.0 * jnp.arange(1, Q_HEADS + 1, dtype=F32) / Q_HEADS),
    )
    return (_trunk(x_prompt, mem_prompt, w, g_final), _trunk(x_sample, mem_sample, w, g_final))
```

```python
import functools

import jax
import jax.numpy as jnp
import numpy as np
from jax import lax
from jax.experimental import pallas as pl
from jax.experimental.pallas import tpu as pltpu

F32 = jnp.float32
BF16 = jnp.bfloat16
I32 = jnp.int32

D_MODEL = 1024
HEAD_DIM = 64
Q_HEADS = 8
KV_HEADS = 2
GROUP = Q_HEADS // KV_HEADS
Q_W = Q_HEADS * HEAD_DIM
KV_W = KV_HEADS * HEAD_DIM
M_HEADS = 4
M_HEAD_DIM = 128
M_W = M_HEADS * M_HEAD_DIM
N_MEM = 256
QKV_W = 2 * (Q_W + 2 * KV_W) + M_W
GATE_W = 3 * D_MODEL
GRID_W = 64
WINDOW = 128
BLOCK = 128
SPAN = BLOCK + 2 * WINDOW
ROPE_THETA = 10000.0
ROPE_HALF = HEAD_DIM // 4
PEER_HEADS = 8
PEER_KEYS = 128
PEER_HALF = 128
PEER_TOPK = 16
PEER_PAIRS = PEER_HEADS * PEER_TOPK
EPS = 1e-6
NEG_INF = -1e30

VMEM_LIMIT_BYTES = 56 * 1024 * 1024
PACK_ROWS = 4
SAFE_LOGIT_BOUND = 40.0
HALF_D = D_MODEL // 2


def _cparams(n_axes):
    return pltpu.CompilerParams(
        dimension_semantics=("arbitrary",) * n_axes,
        vmem_limit_bytes=VMEM_LIMIT_BYTES,
    )


def _rms(x, g):
    ms = jnp.mean(x * x, axis=-1, keepdims=True)
    return (x * lax.rsqrt(ms + EPS)) * g


def _dot(a, b):
    return jnp.dot(a, b, preferred_element_type=F32)


def _dot_nt(a, b):
    return lax.dot_general(a, b, (((1,), (1,)), ((), ())), preferred_element_type=F32)


def _dot_f32_by_01(a, m01):
    hi = a.astype(BF16)
    r1 = a - hi.astype(F32)
    mid = r1.astype(BF16)
    lo = (r1 - mid.astype(F32)).astype(BF16)
    return _dot(lo, m01) + _dot(mid, m01) + _dot(hi, m01)


def _mem_kv_kernel(mem_ref, g_ref, w_ref, mkt_ref, mv_ref):
    h = _rms(mem_ref[0], g_ref[...]).astype(BF16)
    z = _dot(h, w_ref[...])
    mkt_ref[0] = z[:, :M_W].T.astype(BF16)
    mv_ref[0] = z[:, M_W:].astype(BF16)


def _mem_kv(mem, g_mem, w_mkv):
    b = mem.shape[0]
    return pl.pallas_call(
        _mem_kv_kernel,
        grid=(b,),
        in_specs=[
            pl.BlockSpec((1, N_MEM, D_MODEL), lambda i: (i, 0, 0)),
            pl.BlockSpec((1, D_MODEL), lambda i: (0, 0)),
            pl.BlockSpec((D_MODEL, 2 * M_W), lambda i: (0, 0)),
        ],
        out_specs=[
            pl.BlockSpec((1, M_W, N_MEM), lambda i: (i, 0, 0)),
            pl.BlockSpec((1, N_MEM, M_W), lambda i: (i, 0, 0)),
        ],
        out_shape=[
            jax.ShapeDtypeStruct((b, M_W, N_MEM), BF16),
            jax.ShapeDtypeStruct((b, N_MEM, M_W), BF16),
        ],
        compiler_params=_cparams(1),
        name="mem_kv",
    )(mem, g_mem.reshape(1, D_MODEL), w_mkv.astype(BF16))


def _head_norm_rope(q, g, bd, c, s_up, s_dn, width):
    ss = _dot_f32_by_01(q * q, bd)
    qn = (q * lax.rsqrt(ss * (1.0 / HEAD_DIM) + EPS)) * g
    up = pltpu.roll(qn, width - ROPE_HALF, 1)
    dn = pltpu.roll(qn, ROPE_HALF, 1)
    return qn * c + up * s_up + dn * s_dn


def _qkv_kernel(x_ref, g_ref, w_ref, gq_ref, gk_ref, c_ref, su_ref, sd_ref, bdq_ref, bdk_ref,
                qa_ref, kat_ref, va_ref, qb_ref, kb_ref, vb_ref, qm_ref):
    h = _rms(x_ref[0], g_ref[...]).astype(BF16)
    z = _dot(h, w_ref[...])
    c2, su2, sd2 = c_ref[...], su_ref[...], sd_ref[...]
    rep = Q_W // KV_W
    c8 = jnp.concatenate([c2] * rep, axis=1)
    su8 = jnp.concatenate([su2] * rep, axis=1)
    sd8 = jnp.concatenate([sd2] * rep, axis=1)
    o = 0
    qa = _head_norm_rope(z[:, o:o + Q_W], gq_ref[...], bdq_ref[...], c8, su8, sd8, Q_W)
    qa_ref[0] = (qa * (HEAD_DIM ** -0.5)).astype(BF16)
    o += Q_W
    ka = _head_norm_rope(z[:, o:o + KV_W], gk_ref[...], bdk_ref[...], c2, su2, sd2, KV_W)
    kat_ref[0] = ka.T.astype(BF16)
    o += KV_W
    va_ref[0] = z[:, o:o + KV_W].astype(BF16)
    o += KV_W
    qb_ref[0] = (z[:, o:o + Q_W] * (HEAD_DIM ** -0.5)).astype(BF16)
    o += Q_W
    kb_ref[0] = z[:, o:o + KV_W].astype(BF16)
    o += KV_W
    vb_ref[0] = z[:, o:o + KV_W].astype(BF16)
    o += KV_W
    qm_ref[0] = z[:, o:o + M_W].astype(BF16)


def _rope_tables(seq_len):
    rows = seq_len // GRID_W
    row_ids = jnp.repeat(jnp.arange(rows, dtype=F32), GRID_W)
    col_ids = jnp.tile(jnp.arange(GRID_W, dtype=F32), rows)
    inv_freq = ROPE_THETA ** (-jnp.arange(ROPE_HALF, dtype=F32) / ROPE_HALF)
    ang_r = row_ids[:, None] * inv_freq[None, :]
    ang_c = col_ids[:, None] * inv_freq[None, :]
    cr, sr, cc, sc = jnp.cos(ang_r), jnp.sin(ang_r), jnp.cos(ang_c), jnp.sin(ang_c)
    z = jnp.zeros_like(sr)
    c = jnp.concatenate([cr, cr, cc, cc], axis=1)
    s_up = jnp.concatenate([-sr, z, -sc, z], axis=1)
    s_dn = jnp.concatenate([z, sr, z, sc], axis=1)
    two = lambda t: jnp.concatenate([t, t], axis=1)
    return two(c), two(s_up), two(s_dn)


def _block_diag_ones(width):
    i = np.arange(width) // HEAD_DIM
    return jnp.asarray((i[:, None] == i[None, :]).astype(np.float32), dtype=BF16)


def _qkv_proj(x, g_attn, w_qkv, g_qa, g_ka, tm):
    b, s, _ = x.shape
    c, su, sd = _rope_tables(s)
    gq = jnp.tile(g_qa, Q_HEADS).reshape(1, Q_W)
    gk = jnp.tile(g_ka, KV_HEADS).reshape(1, KV_W)
    row = lambda w: pl.BlockSpec((1, tm, w), lambda i, j: (i, j, 0))
    tab = pl.BlockSpec((tm, KV_W), lambda i, j: (j, 0))
    const = lambda shp: pl.BlockSpec(shp, lambda i, j: (0,) * len(shp))
    sds = lambda w: jax.ShapeDtypeStruct((b, s, w), BF16)
    return pl.pallas_call(
        _qkv_kernel,
        grid=(b, s // tm),
        in_specs=[row(D_MODEL), const((1, D_MODEL)), const((D_MODEL, QKV_W)), const((1, Q_W)),
                  const((1, KV_W)), tab, tab, tab, const((Q_W, Q_W)), const((KV_W, KV_W))],
        out_specs=[row(Q_W), pl.BlockSpec((1, KV_W, tm), lambda i, j: (i, 0, j)), row(KV_W),
                   row(Q_W), row(KV_W), row(KV_W), row(M_W)],
        out_shape=[sds(Q_W), jax.ShapeDtypeStruct((b, KV_W, s), BF16), sds(KV_W),
                   sds(Q_W), sds(KV_W), sds(KV_W), sds(M_W)],
        compiler_params=_cparams(2),
        name="qkv_proj",
    )(x, g_attn.reshape(1, D_MODEL), w_qkv, gq, gk, c, su, sd,
      _block_diag_ones(Q_W), _block_diag_ones(KV_W))


def _kv_lane_mask(kvh, dtype):
    lane = lax.broadcasted_iota(I32, (1, KV_W), 1)
    return ((lane // HEAD_DIM) == kvh).astype(dtype)


def _attn_global_kernel(q_ref, kt_ref, v_ref, o_ref, q4_sc, m_sc, l_sc, acc_sc, k2_sm, *, seq, tk):
    tq = q_ref.shape[1]
    rows = GROUP * tq
    n_slab = tk // 128

    @pl.when(pl.program_id(1) == 0)
    def _():
        def key_norms(ci, best):
            off = pl.multiple_of(ci * tk, tk)
            kc = kt_ref[0, :, pl.ds(off, tk)].astype(F32)
            kk = kc * kc
            return tuple(jnp.maximum(best[h], jnp.sum(kk[h * HEAD_DIM:(h + 1) * HEAD_DIM],
                                                      axis=0, keepdims=True))
                         for h in range(KV_HEADS))

        zero = jnp.zeros((1, tk), F32)
        best = lax.fori_loop(0, seq // tk, key_norms, (zero,) * KV_HEADS)
        for h in range(KV_HEADS):
            k2_sm[h] = jnp.max(best[h], axis=1, keepdims=True)[0, 0]

    outs = [None] * Q_HEADS
    for kvh in range(KV_HEADS):
        for gi in range(GROUP):
            head = kvh * GROUP + gi
            qh = q_ref[0, :, head * HEAD_DIM:(head + 1) * HEAD_DIM]
            zeros = jnp.zeros_like(qh)
            q4_sc[gi * tq:(gi + 1) * tq, :] = jnp.concatenate(
                [qh, zeros] if kvh == 0 else [zeros, qh], axis=1)
        q4 = q4_sc[...].astype(F32)
        q2 = jnp.max(jnp.sum(q4 * q4, axis=1, keepdims=True), axis=0, keepdims=True)[0, 0]
        bound = jnp.sqrt(q2 * k2_sm[kvh])

        @pl.when(bound <= SAFE_LOGIT_BOUND)
        def _():
            m_sc[...] = jnp.full((rows, 128), bound, F32)

        @pl.when(bound > SAFE_LOGIT_BOUND)
        def _():
            m_sc[...] = jnp.full((rows, 128), -jnp.inf, F32)

            def row_max(ci, carry):
                off = pl.multiple_of(ci * tk, tk)
                s = _dot(q4_sc[...], kt_ref[0, :, pl.ds(off, tk)])
                m = m_sc[...]
                for j in range(n_slab):
                    m = jnp.maximum(m, s[:, j * 128:(j + 1) * 128])
                m_sc[...] = m
                return carry

            lax.fori_loop(0, seq // tk, row_max, 0)
            m_sc[...] = jnp.broadcast_to(jnp.max(m_sc[...], axis=-1, keepdims=True), (rows, 128))

        l_sc[...] = jnp.zeros((rows, 128), F32)
        acc_sc[...] = jnp.zeros((rows, KV_W), F32)

        def accumulate(ci, carry):
            off = pl.multiple_of(ci * tk, tk)
            s = _dot(q4_sc[...], kt_ref[0, :, pl.ds(off, tk)])
            m = m_sc[...]
            l = l_sc[...]
            ps = []
            for j in range(n_slab):
                pj = jnp.exp(s[:, j * 128:(j + 1) * 128] - m)
                l = l + pj
                ps.append(pj.astype(BF16))
            l_sc[...] = l
            acc_sc[...] += _dot(jnp.concatenate(ps, axis=1), v_ref[0, pl.ds(off, tk), :])
            return carry

        lax.fori_loop(0, seq // tk, accumulate, 0)
        o = acc_sc[...] / jnp.sum(l_sc[...], axis=-1, keepdims=True)
        for gi in range(GROUP):
            outs[kvh * GROUP + gi] = o[gi * tq:(gi + 1) * tq, kvh * HEAD_DIM:(kvh + 1) * HEAD_DIM]
    o_ref[0] = jnp.concatenate(outs, axis=1).astype(BF16)


def _attn_global(qa, kat, va, tq, tk):
    b, s, _ = qa.shape
    return pl.pallas_call(
        functools.partial(_attn_global_kernel, seq=s, tk=tk),
        grid=(b, s // tq),
        in_specs=[
            pl.BlockSpec((1, tq, Q_W), lambda i, j: (i, j, 0)),
            pl.BlockSpec((1, KV_W, s), lambda i, j: (i, 0, 0)),
            pl.BlockSpec((1, s, KV_W), lambda i, j: (i, 0, 0)),
        ],
        out_specs=pl.BlockSpec((1, tq, Q_W), lambda i, j: (i, j, 0)),
        out_shape=jax.ShapeDtypeStruct((b, s, Q_W), BF16),
        scratch_shapes=[pltpu.VMEM((GROUP * tq, KV_W), BF16), pltpu.VMEM((GROUP * tq, 128), F32),
                        pltpu.VMEM((GROUP * tq, 128), F32), pltpu.VMEM((GROUP * tq, KV_W), F32),
                        pltpu.SMEM((KV_HEADS,), F32)],
        compiler_params=_cparams(2),
        name="attn_global",
    )(qa, kat, va)


def _attn_window_kernel(slope_ref, sink_ref, q_ref, k_ref, v_ref, o_ref, *, seq):
    i = pl.program_id(1)
    start = pl.multiple_of(i * BLOCK, BLOCK)
    ks = k_ref[0, pl.ds(start, SPAN), :]
    vs = v_ref[0, pl.ds(start, SPAN), :]
    r = lax.broadcasted_iota(I32, (BLOCK, SPAN), 0)
    c = lax.broadcasted_iota(I32, (BLOCK, SPAN), 1)
    dist = r + WINDOW - c
    adist = jnp.abs(dist)
    key_pos = start - WINDOW + c
    valid = (adist <= WINDOW) & (key_pos >= 0) & (key_pos < seq)
    adist_f = adist.astype(F32)
    outs = [None] * Q_HEADS
    for kvh in range(KV_HEADS):
        heads = range(kvh * GROUP, (kvh + 1) * GROUP)
        qz = []
        for head in heads:
            qh = q_ref[0, :, head * HEAD_DIM:(head + 1) * HEAD_DIM]
            zeros = jnp.zeros_like(qh)
            qz.append(jnp.concatenate([qh, zeros] if kvh == 0 else [zeros, qh], axis=1))
        s = _dot_nt(jnp.concatenate(qz, axis=0), ks)
        logits = jnp.concatenate(
            [jnp.where(valid, s[g * BLOCK:(g + 1) * BLOCK] - slope_ref[head] * adist_f, NEG_INF)
             for g, head in enumerate(heads)], axis=0)
        sink = jnp.concatenate([jnp.full((BLOCK, 1), sink_ref[head], F32) for head in heads], axis=0)
        m = jnp.maximum(jnp.max(logits, axis=-1, keepdims=True), sink)
        e = jnp.exp(logits - m)
        denom = jnp.sum(e, axis=-1, keepdims=True) + jnp.exp(sink - m)
        o = _dot((e / denom).astype(BF16), vs)
        for g, head in enumerate(heads):
            outs[head] = o[g * BLOCK:(g + 1) * BLOCK, kvh * HEAD_DIM:(kvh + 1) * HEAD_DIM]
    o_ref[0] = jnp.concatenate(outs, axis=1).astype(BF16)


def _attn_window(qb, kb, vb, slopes, sink):
    b, s, _ = qb.shape
    pad = ((0, 0), (WINDOW, WINDOW), (0, 0))
    kp, vp = jnp.pad(kb, pad), jnp.pad(vb, pad)
    smem = pl.BlockSpec(memory_space=pltpu.SMEM)
    return pl.pallas_call(
        functools.partial(_attn_window_kernel, seq=s),
        grid=(b, s // BLOCK),
        in_specs=[
            smem, smem,
            pl.BlockSpec((1, BLOCK, Q_W), lambda i, j: (i, j, 0)),
            pl.BlockSpec((1, s + 2 * WINDOW, KV_W), lambda i, j: (i, 0, 0)),
            pl.BlockSpec((1, s + 2 * WINDOW, KV_W), lambda i, j: (i, 0, 0)),
        ],
        out_specs=pl.BlockSpec((1, BLOCK, Q_W), lambda i, j: (i, j, 0)),
        out_shape=jax.ShapeDtypeStruct((b, s, Q_W), BF16),
        compiler_params=_cparams(2),
        name="attn_window",
    )(slopes, sink, qb, kp, vp)


def _attn_mem_kernel(q_ref, mkt_ref, mv_ref, o_ref):
    scale = M_HEAD_DIM ** -0.5
    outs = []
    for head in range(M_HEADS):
        sl = slice(head * M_HEAD_DIM, (head + 1) * M_HEAD_DIM)
        logits = _dot(q_ref[0, :, sl], mkt_ref[0, sl, :]) * scale
        m = jnp.max(logits, axis=-1, keepdims=True)
        e = jnp.exp(logits - m)
        p = (e / jnp.sum(e, axis=-1, keepdims=True)).astype(BF16)
        outs.append(_dot(p, mv_ref[0, :, sl]))
    o_ref[0] = jnp.concatenate(outs, axis=1).astype(BF16)


def _attn_mem(qm, mkt, mv, tq):
    b, s, _ = qm.shape
    return pl.pallas_call(
        _attn_mem_kernel,
        grid=(b, s // tq),
        in_specs=[
            pl.BlockSpec((1, tq, M_W), lambda i, j: (i, j, 0)),
            pl.BlockSpec((1, M_W, N_MEM), lambda i, j: (i, 0, 0)),
            pl.BlockSpec((1, N_MEM, M_W), lambda i, j: (i, 0, 0)),
        ],
        out_specs=pl.BlockSpec((1, tq, M_W), lambda i, j: (i, j, 0)),
        out_shape=jax.ShapeDtypeStruct((b, s, M_W), BF16),
        compiler_params=_cparams(2),
        name="attn_mem",
    )(qm, mkt, mv)


def _merge_kernel(x_ref, g_ref, wg_ref, oa_ref, ob_ref, om_ref, wpa_ref, wpb_ref, wpm_ref, wo_ref,
                  x2_ref):
    x = x_ref[...]
    h = _rms(x, g_ref[...]).astype(BF16)
    gates = jax.nn.sigmoid(_dot(h, wg_ref[...]))
    merged = (gates[:, :D_MODEL] * _dot(oa_ref[...], wpa_ref[...])
              + gates[:, D_MODEL:2 * D_MODEL] * _dot(ob_ref[...], wpb_ref[...])
              + gates[:, 2 * D_MODEL:] * _dot(om_ref[...], wpm_ref[...]))
    x2_ref[...] = x + _dot(merged.astype(BF16), wo_ref[...])


def _merge_proj(x, g_attn, w_gate, oa, ob, om, w_pa, w_pb, w_pm, w_o, tm):
    t = x.shape[0]
    row = lambda w: pl.BlockSpec((tm, w), lambda i: (i, 0))
    const = lambda shp: pl.BlockSpec(shp, lambda i: (0, 0))
    return pl.pallas_call(
        _merge_kernel,
        grid=(t // tm,),
        in_specs=[row(D_MODEL), const((1, D_MODEL)), const((D_MODEL, GATE_W)),
                  row(Q_W), row(Q_W), row(M_W),
                  const((Q_W, D_MODEL)), const((Q_W, D_MODEL)), const((M_W, D_MODEL)),
                  const((D_MODEL, D_MODEL))],
        out_specs=row(D_MODEL),
        out_shape=jax.ShapeDtypeStruct((t, D_MODEL), F32),
        compiler_params=_cparams(1),
        name="merge_proj",
    )(x, g_attn.reshape(1, D_MODEL), w_gate, oa, ob, om, w_pa, w_pb, w_pm, w_o)


def _topk_rows(s, aux, k):
    n = s.shape[0]
    iota = lax.broadcasted_iota(I32, s.shape, 0).astype(F32)
    vals, picks = [], []
    for _ in range(k):
        m = jnp.max(s, axis=0, keepdims=True)
        win = jnp.min(jnp.where(s == m, iota, float(n)), axis=0, keepdims=True)
        sel = iota == win
        vals.append(m)
        if aux is None:
            picks.append(win)
        else:
            picks.append(jnp.max(jnp.where(sel, aux, -1.0), axis=0, keepdims=True))
        s = jnp.where(sel, -jnp.inf, s)
    return jnp.concatenate(vals, axis=0), jnp.concatenate(picks, axis=0)


def _pair_candidates(a, b, combine):
    half = PEER_TOPK // 2
    rows = [combine(a[0:1], b)]
    rows += [combine(a[i:i + 1], b[0:half]) for i in range(1, half)]
    rows.append(combine(a[half:], b[0:1]))
    return jnp.concatenate(rows, axis=0)


def _route_kernel(x2_ref, g_ref, wq_ref, k1_ref, k2_ref, h2_ref, idx_ref, gate_ref):
    h2 = _rms(x2_ref[...], g_ref[...])
    h2_ref[...] = h2
    q = _dot(h2.astype(BF16), wq_ref[...]).astype(BF16)
    idx_rows, gate_rows = [], []
    for head in range(PEER_HEADS):
        o = head * 2 * PEER_HALF
        s1 = _dot_nt(k1_ref[head], q[:, o:o + PEER_HALF])
        s2 = _dot_nt(k2_ref[head], q[:, o + PEER_HALF:o + 2 * PEER_HALF])
        sv1, si1 = _topk_rows(s1, None, PEER_TOPK)
        sv2, si2 = _topk_rows(s2, None, PEER_TOPK)
        cand = _pair_candidates(sv1, sv2, lambda a, b: a + b)
        cidx = _pair_candidates(si1, si2, lambda a, b: a * float(PEER_KEYS) + b)
        score, eidx = _topk_rows(cand, cidx, PEER_TOPK)
        e = jnp.exp(score - score[0:1])
        gate_rows.append(e / jnp.sum(e, axis=0, keepdims=True))
        idx_rows.append(eidx)
    idx = (jnp.concatenate(idx_rows, axis=0) * float(PACK_ROWS)).astype(I32)
    idx_ref[...] = idx.T
    gate_ref[...] = jnp.concatenate(gate_rows, axis=0).T


def _peer_route(x2, g_ffn, w_pq, keys1, keys2, tm):
    t = x2.shape[0]
    row = lambda w: pl.BlockSpec((tm, w), lambda i: (i, 0))
    const = lambda shp: pl.BlockSpec(shp, lambda i: (0,) * len(shp))
    kshape = (PEER_HEADS, PEER_KEYS, PEER_HALF)
    return pl.pallas_call(
        _route_kernel,
        grid=(t // tm,),
        in_specs=[row(D_MODEL), const((1, D_MODEL)), const((D_MODEL, PEER_HEADS * 2 * PEER_HALF)),
                  const(kshape), const(kshape)],
        out_specs=[row(D_MODEL), row(PEER_PAIRS), row(PEER_PAIRS)],
        out_shape=[jax.ShapeDtypeStruct((t, D_MODEL), F32),
                   jax.ShapeDtypeStruct((t, PEER_PAIRS), I32),
                   jax.ShapeDtypeStruct((t, PEER_PAIRS), F32)],
        compiler_params=_cparams(1),
        name="peer_route",
    )(x2, g_ffn.reshape(1, D_MODEL), w_pq, keys1, keys2)


def _pack_table(tab):
    e = tab.shape[0]
    bits = lax.bitcast_convert_type(tab.astype(BF16), jnp.uint16).astype(jnp.uint32)
    words = bits[:, :HALF_D] | (bits[:, HALF_D:] << 16)
    return lax.bitcast_convert_type(words, I32).reshape(e * PACK_ROWS, 128)


def _unpack_row(w):
    lo = lax.bitcast_convert_type(w << 16, F32)
    hi = lax.bitcast_convert_type(w & jnp.int32(-65536), F32)
    return lo, hi


def _table_spec(rows):
    return pl.BlockSpec((rows, 128), lambda i: (0, 0), pipeline_mode=pl.Buffered(1))


ID_GROUP = 8
ID_SLOT = ID_GROUP * PEER_PAIRS


def _for_id_groups(idx_hbm, idx_sm, sem, tm, n_steps, body):
    n_groups = tm // ID_GROUP
    step = pl.program_id(0)
    last_group = n_steps * n_groups - 1
    first = step * n_groups

    def copy(g, slot):
        src = idx_hbm.at[pl.ds(g * ID_SLOT, ID_SLOT)]
        return pltpu.make_async_copy(src, idx_sm[slot], sem.at[slot])

    @pl.when(step == 0)
    def _():
        copy(0, 0).start()

    def group_pair(j, carry):
        g0 = first + 2 * j
        copy(g0 + 1, 1).start()
        copy(g0, 0).wait()
        body(2 * j * ID_GROUP, idx_sm[0])
        copy(jnp.minimum(g0 + 2, last_group), 0).start()
        copy(g0 + 1, 1).wait()
        body((2 * j + 1) * ID_GROUP, idx_sm[1])
        return carry

    lax.fori_loop(0, n_groups // 2, group_pair, 0)

    @pl.when(step == n_steps - 1)
    def _():
        copy(last_group, 0).wait()


def _id_scratch():
    return [pltpu.SMEM((ID_SLOT,), I32), pltpu.SMEM((ID_SLOT,), I32), pltpu.SemaphoreType.DMA((2,))]


def _peer_act_kernel(idx_hbm, h3_ref, gate_ref, tab_ref, coef_ref, part_a, part_b, act_sc,
                     ids_a, ids_b, sem, *, n_steps):
    tm = h3_ref.shape[0]

    def group(t0, ids):
        for s in range(ID_GROUP):
            part_sc = part_b if s % 2 else part_a
            x = h3_ref[t0 + s]
            xlo, xhi = x[0:PACK_ROWS], x[PACK_ROWS:]
            for k in range(PEER_PAIRS):
                row = pl.multiple_of(ids[s * PEER_PAIRS + k], PACK_ROWS)
                lo, hi = _unpack_row(tab_ref[pl.ds(row, PACK_ROWS), :])
                part_sc[k * PACK_ROWS:(k + 1) * PACK_ROWS, :] = lo * xlo + hi * xhi
            part = part_sc[pl.ds(0, PEER_PAIRS, stride=PACK_ROWS), :]
            for r in range(1, PACK_ROWS):
                part = part + part_sc[pl.ds(r, PEER_PAIRS, stride=PACK_ROWS), :]
            act_sc[pl.ds(t0 + s, 1), :] = jnp.sum(part.T, axis=0, keepdims=True)

    _for_id_groups(idx_hbm, (ids_a, ids_b), sem, tm, n_steps, group)
    a = act_sc[...]
    gelu = 0.5 * a * (1.0 + lax.erf(a * (2.0 ** -0.5)))
    coef_ref[...] = gelu * gate_ref[...]


def _peer_act(idx, h3, gate, utab, tm):
    t = h3.shape[0]
    return pl.pallas_call(
        functools.partial(_peer_act_kernel, n_steps=t // tm),
        grid=(t // tm,),
        in_specs=[
            pl.BlockSpec(memory_space=pl.ANY),
            pl.BlockSpec((tm, 8, 128), lambda i: (i, 0, 0)),
            pl.BlockSpec((tm, PEER_PAIRS), lambda i: (i, 0)),
            _table_spec(utab.shape[0]),
        ],
        out_specs=pl.BlockSpec((tm, PEER_PAIRS), lambda i: (i, 0)),
        out_shape=jax.ShapeDtypeStruct((t, PEER_PAIRS), F32),
        scratch_shapes=[pltpu.VMEM((PEER_PAIRS * PACK_ROWS, 128), F32),
                        pltpu.VMEM((PEER_PAIRS * PACK_ROWS, 128), F32),
                        pltpu.VMEM((tm, PEER_PAIRS), F32)] + _id_scratch(),
        compiler_params=_cparams(1),
        name="peer_act",
    )(idx.reshape(-1), h3, gate, utab)


def _peer_out_kernel(idx_hbm, coef_ref, x3_ref, g_ref, tab_ref, y_ref, o_sc, cb_a, cb_b,
                     ids_a, ids_b, sem, *, n_steps):
    tm = x3_ref.shape[0]
    n_acc = 4

    def spread(t, cb_sc):
        cb_sc[...] = jnp.broadcast_to(coef_ref[pl.ds(t, 1), :], (PEER_PAIRS, PEER_PAIRS)).T

    def group(t0, ids):
        spread(t0, cb_a)
        for s in range(ID_GROUP):
            cb_sc = cb_b if s % 2 else cb_a
            if s + 1 < ID_GROUP:
                spread(t0 + s + 1, cb_a if s % 2 else cb_b)
            acc_lo = [jnp.zeros((PACK_ROWS, 128), F32) for _ in range(n_acc)]
            acc_hi = [jnp.zeros((PACK_ROWS, 128), F32) for _ in range(n_acc)]
            for k in range(PEER_PAIRS):
                row = pl.multiple_of(ids[s * PEER_PAIRS + k], PACK_ROWS)
                lo, hi = _unpack_row(tab_ref[pl.ds(row, PACK_ROWS), :])
                c = jnp.broadcast_to(cb_sc[k:k + 1, :], (PACK_ROWS, 128))
                acc_lo[k % n_acc] = acc_lo[k % n_acc] + c * lo
                acc_hi[k % n_acc] = acc_hi[k % n_acc] + c * hi
            lo = (acc_lo[0] + acc_lo[1]) + (acc_lo[2] + acc_lo[3])
            hi = (acc_hi[0] + acc_hi[1]) + (acc_hi[2] + acc_hi[3])
            o_sc[t0 + s] = jnp.concatenate([lo, hi], axis=0)

    _for_id_groups(idx_hbm, (ids_a, ids_b), sem, tm, n_steps, group)
    z = x3_ref[...] + o_sc[...]
    ss = jnp.sum(jnp.sum(z * z, axis=2, keepdims=True), axis=1, keepdims=True)
    y_ref[...] = (z * lax.rsqrt(ss * (1.0 / D_MODEL) + EPS)) * g_ref[...]


def _peer_out(idx, coef, x3, g_final, vtab, tm):
    t = x3.shape[0]
    return pl.pallas_call(
        functools.partial(_peer_out_kernel, n_steps=t // tm),
        grid=(t // tm,),
        in_specs=[
            pl.BlockSpec(memory_space=pl.ANY),
            pl.BlockSpec((tm, PEER_PAIRS), lambda i: (i, 0)),
            pl.BlockSpec((tm, 8, 128), lambda i: (i, 0, 0)),
            pl.BlockSpec((1, 8, 128), lambda i: (0, 0, 0)),
            _table_spec(vtab.shape[0]),
        ],
        out_specs=pl.BlockSpec((tm, 8, 128), lambda i: (i, 0, 0)),
        out_shape=jax.ShapeDtypeStruct((t, 8, 128), F32),
        scratch_shapes=[pltpu.VMEM((tm, 8, 128), F32), pltpu.VMEM((PEER_PAIRS, PEER_PAIRS), F32),
                        pltpu.VMEM((PEER_PAIRS, PEER_PAIRS), F32)] + _id_scratch(),
        compiler_params=_cparams(1),
        name="peer_out",
    )(idx.reshape(-1), coef, x3, g_final.reshape(1, 8, 128), vtab)


def _pick(n, prefs):
    for p in prefs:
        if n % p == 0:
            return p
    raise ValueError(f"no tile in {prefs} divides {n}")


def _trunk(x, mem, w, g_final):
    b, s, _ = x.shape
    t = b * s
    mkt, mv = _mem_kv(mem, w["g_mem"], w["w_mkv"])
    qa, kat, va, qb, kb, vb, qm = _qkv_proj(x, w["g_attn"], w["w_qkv"], w["g_qa"], w["g_ka"],
                                           _pick(s, (512, 256, 128)))
    oa = _attn_global(qa, kat, va, _pick(s, (256, 128)), _pick(s, (512, 256, 128)))
    ob = _attn_window(qb, kb, vb, w["slopes"], w["sink_b"])
    om = _attn_mem(qm, mkt, mv, _pick(s, (512, 256, 128)))
    flat = lambda a: a.reshape(t, a.shape[-1])
    x2 = _merge_proj(flat(x), w["g_attn"], w["w_gate"], flat(oa), flat(ob), flat(om),
                     w["w_pa"], w["w_pb"], w["w_pm"], w["w_o"], _pick(t, (256, 128)))
    h2, idx, gate = _peer_route(x2, w["g_ffn"], w["w_pq"], w["keys1"], w["keys2"],
                                _pick(t, (256, 128)))
    tm = _pick(t, (64, 32, 16, 8))
    coef = _peer_act(idx, h2.reshape(t, 8, 128), gate, w["utab"], tm)
    y3 = _peer_out(idx, coef, x2.reshape(t, 8, 128), g_final, w["vtab"], tm)
    return y3.reshape(b, s, D_MODEL)


def kernel(x_prompt, x_sample, mem_prompt, mem_sample, g_attn, w_in, g_qa, g_ka, sink_b, w_mkv,
           g_mem, w_pa, w_pb, w_pm, w_o, g_ffn, w_pq, peer_keys1, peer_keys2, peer_u, peer_v,
           g_final):
    assert g_attn.shape[0] == 1, "single-layer trunk"
    w_in_b = w_in[0].astype(BF16)
    w = dict(
        g_attn=g_attn[0], w_qkv=w_in_b[:, :QKV_W], w_gate=w_in_b[:, QKV_W:],
        g_qa=g_qa[0], g_ka=g_ka[0], sink_b=sink_b[0], w_mkv=w_mkv[0], g_mem=g_mem[0],
        w_pa=w_pa[0].astype(BF16), w_pb=w_pb[0].astype(BF16), w_pm=w_pm[0].astype(BF16),
        w_o=w_o[0].astype(BF16), g_ffn=g_ffn[0], w_pq=w_pq[0].astype(BF16),
        keys1=peer_keys1[0].astype(BF16), keys2=peer_keys2[0].astype(BF16),
        utab=_pack_table(peer_u[0]), vtab=_pack_table(peer_v[0]),
        slopes=2.0 ** (-8.0 * jnp.arange(1, Q_HEADS + 1, dtype=F32) / Q_HEADS),
    )
    return (_trunk(x_prompt, mem_prompt, w, g_final), _trunk(x_sample, mem_sample, w, g_final))
```

```python
import functools

import jax
import jax.numpy as jnp
import numpy as np
from jax import lax
from jax.experimental import pallas as pl
from jax.experimental.pallas import tpu as pltpu

F32 = jnp.float32
BF16 = jnp.bfloat16
I32 = jnp.int32

D_MODEL = 1024
HEAD_DIM = 64
Q_HEADS = 8
KV_HEADS = 2
GROUP = Q_HEADS // KV_HEADS
Q_W = Q_HEADS * HEAD_DIM
KV_W = KV_HEADS * HEAD_DIM
M_HEADS = 4
M_HEAD_DIM = 128
M_W = M_HEADS * M_HEAD_DIM
N_MEM = 256
QKV_W = 2 * (Q_W + 2 * KV_W) + M_W
GATE_W = 3 * D_MODEL
GRID_W = 64
WINDOW = 128
BLOCK = 128
SPAN = BLOCK + 2 * WINDOW
ROPE_THETA = 10000.0
ROPE_HALF = HEAD_DIM // 4
PEER_HEADS = 8
PEER_KEYS = 128
PEER_HALF = 128
PEER_TOPK = 16
PEER_PAIRS = PEER_HEADS * PEER_TOPK
EPS = 1e-6
NEG_INF = -1e30

VMEM_LIMIT_BYTES = 56 * 1024 * 1024
PACK_ROWS = 4
SAFE_LOGIT_BOUND = 40.0


def _cparams(n_axes):
    return pltpu.CompilerParams(
        dimension_semantics=("arbitrary",) * n_axes,
        vmem_limit_bytes=VMEM_LIMIT_BYTES,
    )


def _rms(x, g):
    ms = jnp.mean(x * x, axis=-1, keepdims=True)
    return (x * lax.rsqrt(ms + EPS)) * g


def _dot(a, b):
    return jnp.dot(a, b, preferred_element_type=F32)


def _dot_nt(a, b):
    return lax.dot_general(a, b, (((1,), (1,)), ((), ())), preferred_element_type=F32)


def _dot_f32_by_01(a, m01):
    hi = a.astype(BF16)
    r1 = a - hi.astype(F32)
    mid = r1.astype(BF16)
    lo = (r1 - mid.astype(F32)).astype(BF16)
    return _dot(lo, m01) + _dot(mid, m01) + _dot(hi, m01)


def _mem_kv_kernel(mem_ref, g_ref, w_ref, mkt_ref, mv_ref):
    h = _rms(mem_ref[0], g_ref[...]).astype(BF16)
    z = _dot(h, w_ref[...])
    mkt_ref[0] = z[:, :M_W].T.astype(BF16)
    mv_ref[0] = z[:, M_W:].astype(BF16)


def _mem_kv(mem, g_mem, w_mkv):
    b = mem.shape[0]
    return pl.pallas_call(
        _mem_kv_kernel,
        grid=(b,),
        in_specs=[
            pl.BlockSpec((1, N_MEM, D_MODEL), lambda i: (i, 0, 0)),
            pl.BlockSpec((1, D_MODEL), lambda i: (0, 0)),
            pl.BlockSpec((D_MODEL, 2 * M_W), lambda i: (0, 0)),
        ],
        out_specs=[
            pl.BlockSpec((1, M_W, N_MEM), lambda i: (i, 0, 0)),
            pl.BlockSpec((1, N_MEM, M_W), lambda i: (i, 0, 0)),
        ],
        out_shape=[
            jax.ShapeDtypeStruct((b, M_W, N_MEM), BF16),
            jax.ShapeDtypeStruct((b, N_MEM, M_W), BF16),
        ],
        compiler_params=_cparams(1),
        name="mem_kv",
    )(mem, g_mem.reshape(1, D_MODEL), w_mkv.astype(BF16))


def _head_norm_rope(q, g, bd, c, s_up, s_dn, width):
    ss = _dot_f32_by_01(q * q, bd)
    qn = (q * lax.rsqrt(ss * (1.0 / HEAD_DIM) + EPS)) * g
    up = pltpu.roll(qn, width - ROPE_HALF, 1)
    dn = pltpu.roll(qn, ROPE_HALF, 1)
    return qn * c + up * s_up + dn * s_dn


def _qkv_kernel(x_ref, g_ref, w_ref, gq_ref, gk_ref, c_ref, su_ref, sd_ref, bdq_ref, bdk_ref,
                qa_ref, kat_ref, va_ref, qb_ref, kb_ref, vb_ref, qm_ref):
    h = _rms(x_ref[0], g_ref[...]).astype(BF16)
    z = _dot(h, w_ref[...])
    c2, su2, sd2 = c_ref[...], su_ref[...], sd_ref[...]
    rep = Q_W // KV_W
    c8 = jnp.concatenate([c2] * rep, axis=1)
    su8 = jnp.concatenate([su2] * rep, axis=1)
    sd8 = jnp.concatenate([sd2] * rep, axis=1)
    o = 0
    qa = _head_norm_rope(z[:, o:o + Q_W], gq_ref[...], bdq_ref[...], c8, su8, sd8, Q_W)
    qa_ref[0] = (qa * (HEAD_DIM ** -0.5)).astype(BF16)
    o += Q_W
    ka = _head_norm_rope(z[:, o:o + KV_W], gk_ref[...], bdk_ref[...], c2, su2, sd2, KV_W)
    kat_ref[0] = ka.T.astype(BF16)
    o += KV_W
    va_ref[0] = z[:, o:o + KV_W].astype(BF16)
    o += KV_W
    qb_ref[0] = (z[:, o:o + Q_W] * (HEAD_DIM ** -0.5)).astype(BF16)
    o += Q_W
    kb_ref[0] = z[:, o:o + KV_W].astype(BF16)
    o += KV_W
    vb_ref[0] = z[:, o:o + KV_W].astype(BF16)
    o += KV_W
    qm_ref[0] = z[:, o:o + M_W].astype(BF16)


def _rope_tables(seq_len):
    rows = seq_len // GRID_W
    row_ids = jnp.repeat(jnp.arange(rows, dtype=F32), GRID_W)
    col_ids = jnp.tile(jnp.arange(GRID_W, dtype=F32), rows)
    inv_freq = ROPE_THETA ** (-jnp.arange(ROPE_HALF, dtype=F32) / ROPE_HALF)
    ang_r = row_ids[:, None] * inv_freq[None, :]
    ang_c = col_ids[:, None] * inv_freq[None, :]
    cr, sr, cc, sc = jnp.cos(ang_r), jnp.sin(ang_r), jnp.cos(ang_c), jnp.sin(ang_c)
    z = jnp.zeros_like(sr)
    c = jnp.concatenate([cr, cr, cc, cc], axis=1)
    s_up = jnp.concatenate([-sr, z, -sc, z], axis=1)
    s_dn = jnp.concatenate([z, sr, z, sc], axis=1)
    two = lambda t: jnp.concatenate([t, t], axis=1)
    return two(c), two(s_up), two(s_dn)


def _block_diag_ones(width):
    i = np.arange(width) // HEAD_DIM
    return jnp.asarray((i[:, None] == i[None, :]).astype(np.float32), dtype=BF16)


def _qkv_proj(x, g_attn, w_qkv, g_qa, g_ka, tm):
    b, s, _ = x.shape
    c, su, sd = _rope_tables(s)
    gq = jnp.tile(g_qa, Q_HEADS).reshape(1, Q_W)
    gk = jnp.tile(g_ka, KV_HEADS).reshape(1, KV_W)
    row = lambda w: pl.BlockSpec((1, tm, w), lambda i, j: (i, j, 0))
    tab = pl.BlockSpec((tm, KV_W), lambda i, j: (j, 0))
    const = lambda shp: pl.BlockSpec(shp, lambda i, j: (0,) * len(shp))
    sds = lambda w: jax.ShapeDtypeStruct((b, s, w), BF16)
    return pl.pallas_call(
        _qkv_kernel,
        grid=(b, s // tm),
        in_specs=[row(D_MODEL), const((1, D_MODEL)), const((D_MODEL, QKV_W)), const((1, Q_W)),
                  const((1, KV_W)), tab, tab, tab, const((Q_W, Q_W)), const((KV_W, KV_W))],
        out_specs=[row(Q_W), pl.BlockSpec((1, KV_W, tm), lambda i, j: (i, 0, j)), row(KV_W),
                   row(Q_W), row(KV_W), row(KV_W), row(M_W)],
        out_shape=[sds(Q_W), jax.ShapeDtypeStruct((b, KV_W, s), BF16), sds(KV_W),
                   sds(Q_W), sds(KV_W), sds(KV_W), sds(M_W)],
        compiler_params=_cparams(2),
        name="qkv_proj",
    )(x, g_attn.reshape(1, D_MODEL), w_qkv, gq, gk, c, su, sd,
      _block_diag_ones(Q_W), _block_diag_ones(KV_W))


def _kv_lane_mask(kvh, dtype):
    lane = lax.broadcasted_iota(I32, (1, KV_W), 1)
    return ((lane // HEAD_DIM) == kvh).astype(dtype)


def _attn_global_kernel(q_ref, kt_ref, v_ref, o_ref, q4_sc, m_sc, l_sc, acc_sc, k2_sm, *, seq, tk):
    tq = q_ref.shape[1]
    rows = GROUP * tq
    n_slab = tk // 128

    @pl.when(pl.program_id(1) == 0)
    def _():
        def key_norms(ci, best):
            off = pl.multiple_of(ci * tk, tk)
            kc = kt_ref[0, :, pl.ds(off, tk)].astype(F32)
            kk = kc * kc
            return tuple(jnp.maximum(best[h], jnp.sum(kk[h * HEAD_DIM:(h + 1) * HEAD_DIM],
                                                      axis=0, keepdims=True))
                         for h in range(KV_HEADS))

        zero = jnp.zeros((1, tk), F32)
        best = lax.fori_loop(0, seq // tk, key_norms, (zero,) * KV_HEADS)
        for h in range(KV_HEADS):
            k2_sm[h] = jnp.max(best[h], axis=1, keepdims=True)[0, 0]

    outs = [None] * Q_HEADS
    for kvh in range(KV_HEADS):
        for gi in range(GROUP):
            head = kvh * GROUP + gi
            qh = q_ref[0, :, head * HEAD_DIM:(head + 1) * HEAD_DIM]
            zeros = jnp.zeros_like(qh)
            q4_sc[gi * tq:(gi + 1) * tq, :] = jnp.concatenate(
                [qh, zeros] if kvh == 0 else [zeros, qh], axis=1)
        q4 = q4_sc[...].astype(F32)
        q2 = jnp.max(jnp.sum(q4 * q4, axis=1, keepdims=True), axis=0, keepdims=True)[0, 0]
        bound = jnp.sqrt(q2 * k2_sm[kvh])

        @pl.when(bound <= SAFE_LOGIT_BOUND)
        def _():
            m_sc[...] = jnp.full((rows, 128), bound, F32)

        @pl.when(bound > SAFE_LOGIT_BOUND)
        def _():
            m_sc[...] = jnp.full((rows, 128), -jnp.inf, F32)

            def row_max(ci, carry):
                off = pl.multiple_of(ci * tk, tk)
                s = _dot(q4_sc[...], kt_ref[0, :, pl.ds(off, tk)])
                m = m_sc[...]
                for j in range(n_slab):
                    m = jnp.maximum(m, s[:, j * 128:(j + 1) * 128])
                m_sc[...] = m
                return carry

            lax.fori_loop(0, seq // tk, row_max, 0)
            m_sc[...] = jnp.broadcast_to(jnp.max(m_sc[...], axis=-1, keepdims=True), (rows, 128))

        l_sc[...] = jnp.zeros((rows, 128), F32)
        acc_sc[...] = jnp.zeros((rows, KV_W), F32)

        def accumulate(ci, carry):
            off = pl.multiple_of(ci * tk, tk)
            s = _dot(q4_sc[...], kt_ref[0, :, pl.ds(off, tk)])
            m = m_sc[...]
            l = l_sc[...]
            ps = []
            for j in range(n_slab):
                pj = jnp.exp(s[:, j * 128:(j + 1) * 128] - m)
                l = l + pj
                ps.append(pj.astype(BF16))
            l_sc[...] = l
            acc_sc[...] += _dot(jnp.concatenate(ps, axis=1), v_ref[0, pl.ds(off, tk), :])
            return carry

        lax.fori_loop(0, seq // tk, accumulate, 0)
        o = acc_sc[...] / jnp.sum(l_sc[...], axis=-1, keepdims=True)
        for gi in range(GROUP):
            outs[kvh * GROUP + gi] = o[gi * tq:(gi + 1) * tq, kvh * HEAD_DIM:(kvh + 1) * HEAD_DIM]
    o_ref[0] = jnp.concatenate(outs, axis=1).astype(BF16)


def _attn_global(qa, kat, va, tq, tk):
    b, s, _ = qa.shape
    return pl.pallas_call(
        functools.partial(_attn_global_kernel, seq=s, tk=tk),
        grid=(b, s // tq),
        in_specs=[
            pl.BlockSpec((1, tq, Q_W), lambda i, j: (i, j, 0)),
            pl.BlockSpec((1, KV_W, s), lambda i, j: (i, 0, 0)),
            pl.BlockSpec((1, s, KV_W), lambda i, j: (i, 0, 0)),
        ],
        out_specs=pl.BlockSpec((1, tq, Q_W), lambda i, j: (i, j, 0)),
        out_shape=jax.ShapeDtypeStruct((b, s, Q_W), BF16),
        scratch_shapes=[pltpu.VMEM((GROUP * tq, KV_W), BF16), pltpu.VMEM((GROUP * tq, 128), F32),
                        pltpu.VMEM((GROUP * tq, 128), F32), pltpu.VMEM((GROUP * tq, KV_W), F32),
                        pltpu.SMEM((KV_HEADS,), F32)],
        compiler_params=_cparams(2),
        name="attn_global",
    )(qa, kat, va)


def _attn_window_kernel(slope_ref, sink_ref, q_ref, k_ref, v_ref, o_ref, *, seq):
    i = pl.program_id(1)
    start = pl.multiple_of(i * BLOCK, BLOCK)
    ks = k_ref[0, pl.ds(start, SPAN), :]
    vs = v_ref[0, pl.ds(start, SPAN), :]
    r = lax.broadcasted_iota(I32, (BLOCK, SPAN), 0)
    c = lax.broadcasted_iota(I32, (BLOCK, SPAN), 1)
    dist = r + WINDOW - c
    adist = jnp.abs(dist)
    key_pos = start - WINDOW + c
    valid = (adist <= WINDOW) & (key_pos >= 0) & (key_pos < seq)
    adist_f = adist.astype(F32)
    outs = [None] * Q_HEADS
    for kvh in range(KV_HEADS):
        heads = range(kvh * GROUP, (kvh + 1) * GROUP)
        qz = []
        for head in heads:
            qh = q_ref[0, :, head * HEAD_DIM:(head + 1) * HEAD_DIM]
            zeros = jnp.zeros_like(qh)
            qz.append(jnp.concatenate([qh, zeros] if kvh == 0 else [zeros, qh], axis=1))
        s = _dot_nt(jnp.concatenate(qz, axis=0), ks)
        logits = jnp.concatenate(
            [jnp.where(valid, s[g * BLOCK:(g + 1) * BLOCK] - slope_ref[head] * adist_f, NEG_INF)
             for g, head in enumerate(heads)], axis=0)
        sink = jnp.concatenate([jnp.full((BLOCK, 1), sink_ref[head], F32) for head in heads], axis=0)
        m = jnp.maximum(jnp.max(logits, axis=-1, keepdims=True), sink)
        e = jnp.exp(logits - m)
        denom = jnp.sum(e, axis=-1, keepdims=True) + jnp.exp(sink - m)
        o = _dot((e / denom).astype(BF16), vs)
        for g, head in enumerate(heads):
            outs[head] = o[g * BLOCK:(g + 1) * BLOCK, kvh * HEAD_DIM:(kvh + 1) * HEAD_DIM]
    o_ref[0] = jnp.concatenate(outs, axis=1).astype(BF16)


def _attn_window(qb, kb, vb, slopes, sink):
    b, s, _ = qb.shape
    pad = ((0, 0), (WINDOW, WINDOW), (0, 0))
    kp, vp = jnp.pad(kb, pad), jnp.pad(vb, pad)
    smem = pl.BlockSpec(memory_space=pltpu.SMEM)
    return pl.pallas_call(
        functools.partial(_attn_window_kernel, seq=s),
        grid=(b, s // BLOCK),
        in_specs=[
            smem, smem,
            pl.BlockSpec((1, BLOCK, Q_W), lambda i, j: (i, j, 0)),
            pl.BlockSpec((1, s + 2 * WINDOW, KV_W), lambda i, j: (i, 0, 0)),
            pl.BlockSpec((1, s + 2 * WINDOW, KV_W), lambda i, j: (i, 0, 0)),
        ],
        out_specs=pl.BlockSpec((1, BLOCK, Q_W), lambda i, j: (i, j, 0)),
        out_shape=jax.ShapeDtypeStruct((b, s, Q_W), BF16),
        compiler_params=_cparams(2),
        name="attn_window",
    )(slopes, sink, qb, kp, vp)


def _attn_mem_kernel(q_ref, mkt_ref, mv_ref, o_ref):
    scale = M_HEAD_DIM ** -0.5
    outs = []
    for head in range(M_HEADS):
        sl = slice(head * M_HEAD_DIM, (head + 1) * M_HEAD_DIM)
        logits = _dot(q_ref[0, :, sl], mkt_ref[0, sl, :]) * scale
        m = jnp.max(logits, axis=-1, keepdims=True)
        e = jnp.exp(logits - m)
        p = (e / jnp.sum(e, axis=-1, keepdims=True)).astype(BF16)
        outs.append(_dot(p, mv_ref[0, :, sl]))
    o_ref[0] = jnp.concatenate(outs, axis=1).astype(BF16)


def _attn_mem(qm, mkt, mv, tq):
    b, s, _ = qm.shape
    return pl.pallas_call(
        _attn_mem_kernel,
        grid=(b, s // tq),
        in_specs=[
            pl.BlockSpec((1, tq, M_W), lambda i, j: (i, j, 0)),
            pl.BlockSpec((1, M_W, N_MEM), lambda i, j: (i, 0, 0)),
            pl.BlockSpec((1, N_MEM, M_W), lambda i, j: (i, 0, 0)),
        ],
        out_specs=pl.BlockSpec((1, tq, M_W), lambda i, j: (i, j, 0)),
        out_shape=jax.ShapeDtypeStruct((b, s, M_W), BF16),
        compiler_params=_cparams(2),
        name="attn_mem",
    )(qm, mkt, mv)


def _merge_kernel(x_ref, g_ref, wg_ref, oa_ref, ob_ref, om_ref, wpa_ref, wpb_ref, wpm_ref, wo_ref,
                  x2_ref):
    x = x_ref[...]
    h = _rms(x, g_ref[...]).astype(BF16)
    gates = jax.nn.sigmoid(_dot(h, wg_ref[...]))
    merged = (gates[:, :D_MODEL] * _dot(oa_ref[...], wpa_ref[...])
              + gates[:, D_MODEL:2 * D_MODEL] * _dot(ob_ref[...], wpb_ref[...])
              + gates[:, 2 * D_MODEL:] * _dot(om_ref[...], wpm_ref[...]))
    x2_ref[...] = x + _dot(merged.astype(BF16), wo_ref[...])


def _merge_proj(x, g_attn, w_gate, oa, ob, om, w_pa, w_pb, w_pm, w_o, tm):
    t = x.shape[0]
    row = lambda w: pl.BlockSpec((tm, w), lambda i: (i, 0))
    const = lambda shp: pl.BlockSpec(shp, lambda i: (0, 0))
    return pl.pallas_call(
        _merge_kernel,
        grid=(t // tm,),
        in_specs=[row(D_MODEL), const((1, D_MODEL)), const((D_MODEL, GATE_W)),
                  row(Q_W), row(Q_W), row(M_W),
                  const((Q_W, D_MODEL)), const((Q_W, D_MODEL)), const((M_W, D_MODEL)),
                  const((D_MODEL, D_MODEL))],
        out_specs=row(D_MODEL),
        out_shape=jax.ShapeDtypeStruct((t, D_MODEL), F32),
        compiler_params=_cparams(1),
        name="merge_proj",
    )(x, g_attn.reshape(1, D_MODEL), w_gate, oa, ob, om, w_pa, w_pb, w_pm, w_o)


def _topk_rows(s, aux, k):
    n = s.shape[0]
    iota = lax.broadcasted_iota(I32, s.shape, 0).astype(F32)
    vals, picks = [], []
    for _ in range(k):
        m = jnp.max(s, axis=0, keepdims=True)
        win = jnp.min(jnp.where(s == m, iota, float(n)), axis=0, keepdims=True)
        sel = iota == win
        vals.append(m)
        if aux is None:
            picks.append(win)
        else:
            picks.append(jnp.max(jnp.where(sel, aux, -1.0), axis=0, keepdims=True))
        s = jnp.where(sel, -jnp.inf, s)
    return jnp.concatenate(vals, axis=0), jnp.concatenate(picks, axis=0)


def _pair_candidates(a, b, combine):
    half = PEER_TOPK // 2
    rows = [combine(a[0:1], b)]
    rows += [combine(a[i:i + 1], b[0:half]) for i in range(1, half)]
    rows.append(combine(a[half:], b[0:1]))
    return jnp.concatenate(rows, axis=0)


def _route_kernel(x2_ref, g_ref, wq_ref, k1_ref, k2_ref, h2_ref, idx_ref, gate_ref):
    h2 = _rms(x2_ref[...], g_ref[...])
    h2_ref[...] = h2
    q = _dot(h2.astype(BF16), wq_ref[...]).astype(BF16)
    idx_rows, gate_rows = [], []
    for head in range(PEER_HEADS):
        o = head * 2 * PEER_HALF
        s1 = _dot_nt(k1_ref[head], q[:, o:o + PEER_HALF])
        s2 = _dot_nt(k2_ref[head], q[:, o + PEER_HALF:o + 2 * PEER_HALF])
        sv1, si1 = _topk_rows(s1, None, PEER_TOPK)
        sv2, si2 = _topk_rows(s2, None, PEER_TOPK)
        cand = _pair_candidates(sv1, sv2, lambda a, b: a + b)
        cidx = _pair_candidates(si1, si2, lambda a, b: a * float(PEER_KEYS) + b)
        score, eidx = _topk_rows(cand, cidx, PEER_TOPK)
        e = jnp.exp(score - score[0:1])
        gate_rows.append(e / jnp.sum(e, axis=0, keepdims=True))
        idx_rows.append(eidx)
    idx = (jnp.concatenate(idx_rows, axis=0) * float(PACK_ROWS)).astype(I32)
    idx_ref[...] = idx.T
    gate_ref[...] = jnp.concatenate(gate_rows, axis=0).T


def _peer_route(x2, g_ffn, w_pq, keys1, keys2, tm):
    t = x2.shape[0]
    row = lambda w: pl.BlockSpec((tm, w), lambda i: (i, 0))
    const = lambda shp: pl.BlockSpec(shp, lambda i: (0,) * len(shp))
    kshape = (PEER_HEADS, PEER_KEYS, PEER_HALF)
    return pl.pallas_call(
        _route_kernel,
        grid=(t // tm,),
        in_specs=[row(D_MODEL), const((1, D_MODEL)), const((D_MODEL, PEER_HEADS * 2 * PEER_HALF)),
                  const(kshape), const(kshape)],
        out_specs=[row(D_MODEL), row(PEER_PAIRS), row(PEER_PAIRS)],
        out_shape=[jax.ShapeDtypeStruct((t, D_MODEL), F32),
                   jax.ShapeDtypeStruct((t, PEER_PAIRS), I32),
                   jax.ShapeDtypeStruct((t, PEER_PAIRS), F32)],
        compiler_params=_cparams(1),
        name="peer_route",
    )(x2, g_ffn.reshape(1, D_MODEL), w_pq, keys1, keys2)


def _pack_table(tab):
    e = tab.shape[0]
    bits = lax.bitcast_convert_type(tab.astype(BF16), jnp.uint16).astype(jnp.uint32)
    bits = bits.reshape(e, PACK_ROWS, 2, 128)
    words = bits[:, :, 0, :] | (bits[:, :, 1, :] << 16)
    return lax.bitcast_convert_type(words, I32).reshape(e * PACK_ROWS, 128)


def _table_spec(rows):
    return pl.BlockSpec((rows, 128), lambda i: (0, 0), pipeline_mode=pl.Buffered(1))


ID_GROUP = 8
ID_SLOT = ID_GROUP * PEER_PAIRS


def _for_id_groups(idx_hbm, idx_sm, sem, tm, n_steps, body):
    n_groups = tm // ID_GROUP
    step = pl.program_id(0)
    last_group = n_steps * n_groups - 1
    first = step * n_groups

    def copy(g, slot):
        src = idx_hbm.at[pl.ds(g * ID_SLOT, ID_SLOT)]
        return pltpu.make_async_copy(src, idx_sm[slot], sem.at[slot])

    @pl.when(step == 0)
    def _():
        copy(0, 0).start()

    def group_pair(j, carry):
        g0 = first + 2 * j
        copy(g0 + 1, 1).start()
        copy(g0, 0).wait()
        body(2 * j * ID_GROUP, idx_sm[0])
        copy(jnp.minimum(g0 + 2, last_group), 0).start()
        copy(g0 + 1, 1).wait()
        body((2 * j + 1) * ID_GROUP, idx_sm[1])
        return carry

    lax.fori_loop(0, n_groups // 2, group_pair, 0)

    @pl.when(step == n_steps - 1)
    def _():
        copy(last_group, 0).wait()


def _id_scratch():
    return [pltpu.SMEM((ID_SLOT,), I32), pltpu.SMEM((ID_SLOT,), I32), pltpu.SemaphoreType.DMA((2,))]


def _peer_act_kernel(idx_hbm, h3_ref, gate_ref, tab_ref, coef_ref, part_a, part_b, act_sc,
                     ids_a, ids_b, sem, *, n_steps):
    tm = h3_ref.shape[0]

    def group(t0, ids):
        for s in range(ID_GROUP):
            part_sc = part_b if s % 2 else part_a
            x = h3_ref[t0 + s].astype(BF16)
            for k in range(PEER_PAIRS):
                row = pl.multiple_of(ids[s * PEER_PAIRS + k], PACK_ROWS)
                w = pltpu.bitcast(tab_ref[pl.ds(row, PACK_ROWS), :], BF16)
                prod = (w * x).astype(F32)
                part_sc[k * PACK_ROWS:(k + 1) * PACK_ROWS, :] = prod[0:PACK_ROWS] + prod[PACK_ROWS:]
            part = part_sc[pl.ds(0, PEER_PAIRS, stride=PACK_ROWS), :]
            for r in range(1, PACK_ROWS):
                part = part + part_sc[pl.ds(r, PEER_PAIRS, stride=PACK_ROWS), :]
            act_sc[pl.ds(t0 + s, 1), :] = jnp.sum(part.T, axis=0, keepdims=True)

    _for_id_groups(idx_hbm, (ids_a, ids_b), sem, tm, n_steps, group)
    a = act_sc[...]
    gelu = 0.5 * a * (1.0 + lax.erf(a * (2.0 ** -0.5)))
    coef_ref[...] = gelu * gate_ref[...]


def _peer_act(idx, h3, gate, utab, tm):
    t = h3.shape[0]
    return pl.pallas_call(
        functools.partial(_peer_act_kernel, n_steps=t // tm),
        grid=(t // tm,),
        in_specs=[
            pl.BlockSpec(memory_space=pl.ANY),
            pl.BlockSpec((tm, 8, 128), lambda i: (i, 0, 0)),
            pl.BlockSpec((tm, PEER_PAIRS), lambda i: (i, 0)),
            _table_spec(utab.shape[0]),
        ],
        out_specs=pl.BlockSpec((tm, PEER_PAIRS), lambda i: (i, 0)),
        out_shape=jax.ShapeDtypeStruct((t, PEER_PAIRS), F32),
        scratch_shapes=[pltpu.VMEM((PEER_PAIRS * PACK_ROWS, 128), F32),
                        pltpu.VMEM((PEER_PAIRS * PACK_ROWS, 128), F32),
                        pltpu.VMEM((tm, PEER_PAIRS), F32)] + _id_scratch(),
        compiler_params=_cparams(1),
        name="peer_act",
    )(idx.reshape(-1), h3, gate, utab)


def _peer_out_kernel(idx_hbm, coef_ref, x3_ref, g_ref, tab_ref, y_ref, o_sc, cb_a, cb_b,
                     ids_a, ids_b, sem, *, n_steps):
    tm = x3_ref.shape[0]
    n_acc = 4

    def spread(t, cb_sc):
        c = coef_ref[pl.ds(t, 1), :].astype(BF16).astype(F32)
        bits = lax.bitcast_convert_type(c, I32)
        words = bits | lax.shift_right_logical(bits, 16)
        cb_sc[...] = jnp.broadcast_to(words, (PEER_PAIRS, PEER_PAIRS)).T

    def group(t0, ids):
        spread(t0, cb_a)
        for s in range(ID_GROUP):
            cb_sc = cb_b if s % 2 else cb_a
            if s + 1 < ID_GROUP:
                spread(t0 + s + 1, cb_a if s % 2 else cb_b)
            acc = [jnp.zeros((8, 128), F32) for _ in range(n_acc)]
            for k in range(PEER_PAIRS):
                row = pl.multiple_of(ids[s * PEER_PAIRS + k], PACK_ROWS)
                w = pltpu.bitcast(tab_ref[pl.ds(row, PACK_ROWS), :], BF16)
                c = pltpu.bitcast(jnp.broadcast_to(cb_sc[k:k + 1, :], (PACK_ROWS, 128)), BF16)
                acc[k % n_acc] = acc[k % n_acc] + (c * w).astype(F32)
            o_sc[t0 + s] = (acc[0] + acc[1]) + (acc[2] + acc[3])

    _for_id_groups(idx_hbm, (ids_a, ids_b), sem, tm, n_steps, group)
    z = x3_ref[...] + o_sc[...]
    ss = jnp.sum(jnp.sum(z * z, axis=2, keepdims=True), axis=1, keepdims=True)
    y_ref[...] = (z * lax.rsqrt(ss * (1.0 / D_MODEL) + EPS)) * g_ref[...]


def _peer_out(idx, coef, x3, g_final, vtab, tm):
    t = x3.shape[0]
    return pl.pallas_call(
        functools.partial(_peer_out_kernel, n_steps=t // tm),
        grid=(t // tm,),
        in_specs=[
            pl.BlockSpec(memory_space=pl.ANY),
            pl.BlockSpec((tm, PEER_PAIRS), lambda i: (i, 0)),
            pl.BlockSpec((tm, 8, 128), lambda i: (i, 0, 0)),
            pl.BlockSpec((1, 8, 128), lambda i: (0, 0, 0)),
            _table_spec(vtab.shape[0]),
        ],
        out_specs=pl.BlockSpec((tm, 8, 128), lambda i: (i, 0, 0)),
        out_shape=jax.ShapeDtypeStruct((t, 8, 128), F32),
        scratch_shapes=[pltpu.VMEM((tm, 8, 128), F32), pltpu.VMEM((PEER_PAIRS, PEER_PAIRS), I32),
                        pltpu.VMEM((PEER_PAIRS, PEER_PAIRS), I32)] + _id_scratch(),
        compiler_params=_cparams(1),
        name="peer_out",
    )(idx.reshape(-1), coef, x3, g_final.reshape(1, 8, 128), vtab)


def _pick(n, prefs):
    for p in prefs:
        if n % p == 0:
            return p
    raise ValueError(f"no tile in {prefs} divides {n}")


def _trunk(x, mem, w, g_final):
    b, s, _ = x.shape
    t = b * s
    mkt, mv = _mem_kv(mem, w["g_mem"], w["w_mkv"])
    qa, kat, va, qb, kb, vb, qm = _qkv_proj(x, w["g_attn"], w["w_qkv"], w["g_qa"], w["g_ka"],
                                           _pick(s, (512, 256, 128)))
    oa = _attn_global(qa, kat, va, _pick(s, (256, 128)), _pick(s, (512, 256, 128)))
    ob = _attn_window(qb, kb, vb, w["slopes"], w["sink_b"])
    om = _attn_mem(qm, mkt, mv, _pick(s, (512, 256, 128)))
    flat = lambda a: a.reshape(t, a.shape[-1])
    x2 = _merge_proj(flat(x), w["g_attn"], w["w_gate"], flat(oa), flat(ob), flat(om),
                     w["w_pa"], w["w_pb"], w["w_pm"], w["w_o"], _pick(t, (256, 128)))
    h2, idx, gate = _peer_route(x2, w["g_ffn"], w["w_pq"], w["keys1"], w["keys2"],
                                _pick(t, (256, 128)))
    tm = _pick(t, (64, 32, 16, 8))
    coef = _peer_act(idx, h2.reshape(t, 8, 128), gate, w["utab"], tm)
    y3 = _peer_out(idx, coef, x2.reshape(t, 8, 128), g_final, w["vtab"], tm)
    return y3.reshape(b, s, D_MODEL)


def kernel(x_prompt, x_sample, mem_prompt, mem_sample, g_attn, w_in, g_qa, g_ka, sink_b, w_mkv,
           g_mem, w_pa, w_pb, w_pm, w_o, g_ffn, w_pq, peer_keys1, peer_keys2, peer_u, peer_v,
           g_final):
    assert g_attn.shape[0] == 1, "single-layer trunk"
    w_in_b = w_in[0].astype(BF16)
    w = dict(
        g_attn=g_attn[0], w_qkv=w_in_b[:, :QKV_W], w_gate=w_in_b[:, QKV_W:],
        g_qa=g_qa[0], g_ka=g_ka[0], sink_b=sink_b[0], w_mkv=w_mkv[0], g_mem=g_mem[0],
        w_pa=w_pa[0].astype(BF16), w_pb=w_pb[0].astype(BF16), w_pm=w_pm[0].astype(BF16),
        w_o=w_o[0].astype(BF16), g_ffn=g_ffn[0], w_pq=w_pq[0].astype(BF16),
        keys1=peer_keys1[0].astype(BF16), keys2=peer_keys2[0].astype(BF16),
        utab=_pack_table(peer_u[0]), vtab=_pack_table(peer_v[0]),
        slopes=2.0 ** (-8.0 * jnp.arange(1, Q_HEADS + 1, dtype=F32) / Q_HEADS),
    )
    return (_trunk(x_prompt, mem_prompt, w, g_final), _trunk(x_sample, mem_sample, w, g_final))
```

```python
import functools

import jax
import jax.numpy as jnp
import numpy as np
from jax import lax
from jax.experimental import pallas as pl
from jax.experimental.pallas import tpu as pltpu

F32 = jnp.float32
BF16 = jnp.bfloat16
I32 = jnp.int32

D_MODEL = 1024
HEAD_DIM = 64
Q_HEADS = 8
KV_HEADS = 2
GROUP = Q_HEADS // KV_HEADS
Q_W = Q_HEADS * HEAD_DIM
KV_W = KV_HEADS * HEAD_DIM
M_HEADS = 4
M_HEAD_DIM = 128
M_W = M_HEADS * M_HEAD_DIM
N_MEM = 256
QKV_W = 2 * (Q_W + 2 * KV_W) + M_W
GATE_W = 3 * D_MODEL
GRID_W = 64
WINDOW = 128
BLOCK = 128
SPAN = BLOCK + 2 * WINDOW
ROPE_THETA = 10000.0
ROPE_HALF = HEAD_DIM // 4
PEER_HEADS = 8
PEER_KEYS = 128
PEER_HALF = 128
PEER_TOPK = 16
PEER_PAIRS = PEER_HEADS * PEER_TOPK
EPS = 1e-6
NEG_INF = -1e30

VMEM_LIMIT_BYTES = 56 * 1024 * 1024
PACK_ROWS = 4
SAFE_LOGIT_BOUND = 40.0


def _cparams(n_axes):
    return pltpu.CompilerParams(
        dimension_semantics=("arbitrary",) * n_axes,
        vmem_limit_bytes=VMEM_LIMIT_BYTES,
    )


def _rms(x, g):
    ms = jnp.mean(x * x, axis=-1, keepdims=True)
    return (x * lax.rsqrt(ms + EPS)) * g


def _dot(a, b):
    return jnp.dot(a, b, preferred_element_type=F32)


def _dot_nt(a, b):
    return lax.dot_general(a, b, (((1,), (1,)), ((), ())), preferred_element_type=F32)


def _dot_f32_by_01(a, m01):
    hi = a.astype(BF16)
    r1 = a - hi.astype(F32)
    mid = r1.astype(BF16)
    lo = (r1 - mid.astype(F32)).astype(BF16)
    return _dot(lo, m01) + _dot(mid, m01) + _dot(hi, m01)


def _mem_kv_kernel(mem_ref, g_ref, w_ref, mkt_ref, mv_ref):
    h = _rms(mem_ref[0], g_ref[...]).astype(BF16)
    z = _dot(h, w_ref[...])
    mkt_ref[0] = z[:, :M_W].T.astype(BF16)
    mv_ref[0] = z[:, M_W:].astype(BF16)


def _mem_kv(mem, g_mem, w_mkv):
    b = mem.shape[0]
    return pl.pallas_call(
        _mem_kv_kernel,
        grid=(b,),
        in_specs=[
            pl.BlockSpec((1, N_MEM, D_MODEL), lambda i: (i, 0, 0)),
            pl.BlockSpec((1, D_MODEL), lambda i: (0, 0)),
            pl.BlockSpec((D_MODEL, 2 * M_W), lambda i: (0, 0)),
        ],
        out_specs=[
            pl.BlockSpec((1, M_W, N_MEM), lambda i: (i, 0, 0)),
            pl.BlockSpec((1, N_MEM, M_W), lambda i: (i, 0, 0)),
        ],
        out_shape=[
            jax.ShapeDtypeStruct((b, M_W, N_MEM), BF16),
            jax.ShapeDtypeStruct((b, N_MEM, M_W), BF16),
        ],
        compiler_params=_cparams(1),
        name="mem_kv",
    )(mem, g_mem.reshape(1, D_MODEL), w_mkv.astype(BF16))


def _head_norm_rope(q, g, bd, c, s_up, s_dn, width):
    ss = _dot_f32_by_01(q * q, bd)
    qn = (q * lax.rsqrt(ss * (1.0 / HEAD_DIM) + EPS)) * g
    up = pltpu.roll(qn, width - ROPE_HALF, 1)
    dn = pltpu.roll(qn, ROPE_HALF, 1)
    return qn * c + up * s_up + dn * s_dn


def _qkv_kernel(x_ref, g_ref, w_ref, gq_ref, gk_ref, c_ref, su_ref, sd_ref, bdq_ref, bdk_ref,
                qa_ref, kat_ref, va_ref, qb_ref, kb_ref, vb_ref, qm_ref):
    h = _rms(x_ref[0], g_ref[...]).astype(BF16)
    z = _dot(h, w_ref[...])
    c2, su2, sd2 = c_ref[...], su_ref[...], sd_ref[...]
    rep = Q_W // KV_W
    c8 = jnp.concatenate([c2] * rep, axis=1)
    su8 = jnp.concatenate([su2] * rep, axis=1)
    sd8 = jnp.concatenate([sd2] * rep, axis=1)
    o = 0
    qa = _head_norm_rope(z[:, o:o + Q_W], gq_ref[...], bdq_ref[...], c8, su8, sd8, Q_W)
    qa_ref[0] = (qa * (HEAD_DIM ** -0.5)).astype(BF16)
    o += Q_W
    ka = _head_norm_rope(z[:, o:o + KV_W], gk_ref[...], bdk_ref[...], c2, su2, sd2, KV_W)
    kat_ref[0] = ka.T.astype(BF16)
    o += KV_W
    va_ref[0] = z[:, o:o + KV_W].astype(BF16)
    o += KV_W
    qb_ref[0] = (z[:, o:o + Q_W] * (HEAD_DIM ** -0.5)).astype(BF16)
    o += Q_W
    kb_ref[0] = z[:, o:o + KV_W].astype(BF16)
    o += KV_W
    vb_ref[0] = z[:, o:o + KV_W].astype(BF16)
    o += KV_W
    qm_ref[0] = z[:, o:o + M_W].astype(BF16)


def _rope_tables(seq_len):
    rows = seq_len // GRID_W
    row_ids = jnp.repeat(jnp.arange(rows, dtype=F32), GRID_W)
    col_ids = jnp.tile(jnp.arange(GRID_W, dtype=F32), rows)
    inv_freq = ROPE_THETA ** (-jnp.arange(ROPE_HALF, dtype=F32) / ROPE_HALF)
    ang_r = row_ids[:, None] * inv_freq[None, :]
    ang_c = col_ids[:, None] * inv_freq[None, :]
    cr, sr, cc, sc = jnp.cos(ang_r), jnp.sin(ang_r), jnp.cos(ang_c), jnp.sin(ang_c)
    z = jnp.zeros_like(sr)
    c = jnp.concatenate([cr, cr, cc, cc], axis=1)
    s_up = jnp.concatenate([-sr, z, -sc, z], axis=1)
    s_dn = jnp.concatenate([z, sr, z, sc], axis=1)
    two = lambda t: jnp.concatenate([t, t], axis=1)
    return two(c), two(s_up), two(s_dn)


def _block_diag_ones(width):
    i = np.arange(width) // HEAD_DIM
    return jnp.asarray((i[:, None] == i[None, :]).astype(np.float32), dtype=BF16)


def _qkv_proj(x, g_attn, w_qkv, g_qa, g_ka, tm):
    b, s, _ = x.shape
    c, su, sd = _rope_tables(s)
    gq = jnp.tile(g_qa, Q_HEADS).reshape(1, Q_W)
    gk = jnp.tile(g_ka, KV_HEADS).reshape(1, KV_W)
    row = lambda w: pl.BlockSpec((1, tm, w), lambda i, j: (i, j, 0))
    tab = pl.BlockSpec((tm, KV_W), lambda i, j: (j, 0))
    const = lambda shp: pl.BlockSpec(shp, lambda i, j: (0,) * len(shp))
    sds = lambda w: jax.ShapeDtypeStruct((b, s, w), BF16)
    return pl.pallas_call(
        _qkv_kernel,
        grid=(b, s // tm),
        in_specs=[row(D_MODEL), const((1, D_MODEL)), const((D_MODEL, QKV_W)), const((1, Q_W)),
                  const((1, KV_W)), tab, tab, tab, const((Q_W, Q_W)), const((KV_W, KV_W))],
        out_specs=[row(Q_W), pl.BlockSpec((1, KV_W, tm), lambda i, j: (i, 0, j)), row(KV_W),
                   row(Q_W), row(KV_W), row(KV_W), row(M_W)],
        out_shape=[sds(Q_W), jax.ShapeDtypeStruct((b, KV_W, s), BF16), sds(KV_W),
                   sds(Q_W), sds(KV_W), sds(KV_W), sds(M_W)],
        compiler_params=_cparams(2),
        name="qkv_proj",
    )(x, g_attn.reshape(1, D_MODEL), w_qkv, gq, gk, c, su, sd,
      _block_diag_ones(Q_W), _block_diag_ones(KV_W))


def _kv_lane_mask(kvh, dtype):
    lane = lax.broadcasted_iota(I32, (1, KV_W), 1)
    return ((lane // HEAD_DIM) == kvh).astype(dtype)


def _attn_global_kernel(q_ref, kt_ref, v_ref, o_ref, q4_sc, m_sc, l_sc, acc_sc, k2_sm, *, seq, tk):
    tq = q_ref.shape[1]
    rows = GROUP * tq
    n_slab = tk // 128

    @pl.when(pl.program_id(1) == 0)
    def _():
        def key_norms(ci, best):
            off = pl.multiple_of(ci * tk, tk)
            kc = kt_ref[0, :, pl.ds(off, tk)].astype(F32)
            kk = kc * kc
            return tuple(jnp.maximum(best[h], jnp.sum(kk[h * HEAD_DIM:(h + 1) * HEAD_DIM],
                                                      axis=0, keepdims=True))
                         for h in range(KV_HEADS))

        zero = jnp.zeros((1, tk), F32)
        best = lax.fori_loop(0, seq // tk, key_norms, (zero,) * KV_HEADS)
        for h in range(KV_HEADS):
            k2_sm[h] = jnp.max(best[h], axis=1, keepdims=True)[0, 0]

    outs = [None] * Q_HEADS
    for kvh in range(KV_HEADS):
        for gi in range(GROUP):
            head = kvh * GROUP + gi
            qh = q_ref[0, :, head * HEAD_DIM:(head + 1) * HEAD_DIM]
            zeros = jnp.zeros_like(qh)
            q4_sc[gi * tq:(gi + 1) * tq, :] = jnp.concatenate(
                [qh, zeros] if kvh == 0 else [zeros, qh], axis=1)
        q4 = q4_sc[...].astype(F32)
        q2 = jnp.max(jnp.sum(q4 * q4, axis=1, keepdims=True), axis=0, keepdims=True)[0, 0]
        bound = jnp.sqrt(q2 * k2_sm[kvh])

        @pl.when(bound <= SAFE_LOGIT_BOUND)
        def _():
            m_sc[...] = jnp.full((rows, 128), bound, F32)

        @pl.when(bound > SAFE_LOGIT_BOUND)
        def _():
            m_sc[...] = jnp.full((rows, 128), -jnp.inf, F32)

            def row_max(ci, carry):
                off = pl.multiple_of(ci * tk, tk)
                s = _dot(q4_sc[...], kt_ref[0, :, pl.ds(off, tk)])
                m = m_sc[...]
                for j in range(n_slab):
                    m = jnp.maximum(m, s[:, j * 128:(j + 1) * 128])
                m_sc[...] = m
                return carry

            lax.fori_loop(0, seq // tk, row_max, 0)
            m_sc[...] = jnp.broadcast_to(jnp.max(m_sc[...], axis=-1, keepdims=True), (rows, 128))

        l_sc[...] = jnp.zeros((rows, 128), F32)
        acc_sc[...] = jnp.zeros((rows, KV_W), F32)

        def accumulate(ci, carry):
            off = pl.multiple_of(ci * tk, tk)
            s = _dot(q4_sc[...], kt_ref[0, :, pl.ds(off, tk)])
            m = m_sc[...]
            l = l_sc[...]
            ps = []
            for j in range(n_slab):
                pj = jnp.exp(s[:, j * 128:(j + 1) * 128] - m)
                l = l + pj
                ps.append(pj.astype(BF16))
            l_sc[...] = l
            acc_sc[...] += _dot(jnp.concatenate(ps, axis=1), v_ref[0, pl.ds(off, tk), :])
            return carry

        lax.fori_loop(0, seq // tk, accumulate, 0)
        o = acc_sc[...] / jnp.sum(l_sc[...], axis=-1, keepdims=True)
        for gi in range(GROUP):
            outs[kvh * GROUP + gi] = o[gi * tq:(gi + 1) * tq, kvh * HEAD_DIM:(kvh + 1) * HEAD_DIM]
    o_ref[0] = jnp.concatenate(outs, axis=1).astype(BF16)


def _attn_global(qa, kat, va, tq, tk):
    b, s, _ = qa.shape
    return pl.pallas_call(
        functools.partial(_attn_global_kernel, seq=s, tk=tk),
        grid=(b, s // tq),
        in_specs=[
            pl.BlockSpec((1, tq, Q_W), lambda i, j: (i, j, 0)),
            pl.BlockSpec((1, KV_W, s), lambda i, j: (i, 0, 0)),
            pl.BlockSpec((1, s, KV_W), lambda i, j: (i, 0, 0)),
        ],
        out_specs=pl.BlockSpec((1, tq, Q_W), lambda i, j: (i, j, 0)),
        out_shape=jax.ShapeDtypeStruct((b, s, Q_W), BF16),
        scratch_shapes=[pltpu.VMEM((GROUP * tq, KV_W), BF16), pltpu.VMEM((GROUP * tq, 128), F32),
                        pltpu.VMEM((GROUP * tq, 128), F32), pltpu.VMEM((GROUP * tq, KV_W), F32),
                        pltpu.SMEM((KV_HEADS,), F32)],
        compiler_params=_cparams(2),
        name="attn_global",
    )(qa, kat, va)


def _attn_window_kernel(slope_ref, sink_ref, q_ref, k_ref, v_ref, o_ref, *, seq):
    i = pl.program_id(1)
    start = pl.multiple_of(i * BLOCK, BLOCK)
    ks = k_ref[0, pl.ds(start, SPAN), :]
    vs = v_ref[0, pl.ds(start, SPAN), :]
    r = lax.broadcasted_iota(I32, (BLOCK, SPAN), 0)
    c = lax.broadcasted_iota(I32, (BLOCK, SPAN), 1)
    dist = r + WINDOW - c
    adist = jnp.abs(dist)
    key_pos = start - WINDOW + c
    valid = (adist <= WINDOW) & (key_pos >= 0) & (key_pos < seq)
    adist_f = adist.astype(F32)
    outs = [None] * Q_HEADS
    for kvh in range(KV_HEADS):
        heads = range(kvh * GROUP, (kvh + 1) * GROUP)
        qz = []
        for head in heads:
            qh = q_ref[0, :, head * HEAD_DIM:(head + 1) * HEAD_DIM]
            zeros = jnp.zeros_like(qh)
            qz.append(jnp.concatenate([qh, zeros] if kvh == 0 else [zeros, qh], axis=1))
        s = _dot_nt(jnp.concatenate(qz, axis=0), ks)
        logits = jnp.concatenate(
            [jnp.where(valid, s[g * BLOCK:(g + 1) * BLOCK] - slope_ref[head] * adist_f, NEG_INF)
             for g, head in enumerate(heads)], axis=0)
        sink = jnp.concatenate([jnp.full((BLOCK, 1), sink_ref[head], F32) for head in heads], axis=0)
        m = jnp.maximum(jnp.max(logits, axis=-1, keepdims=True), sink)
        e = jnp.exp(logits - m)
        denom = jnp.sum(e, axis=-1, keepdims=True) + jnp.exp(sink - m)
        o = _dot((e / denom).astype(BF16), vs)
        for g, head in enumerate(heads):
            outs[head] = o[g * BLOCK:(g + 1) * BLOCK, kvh * HEAD_DIM:(kvh + 1) * HEAD_DIM]
    o_ref[0] = jnp.concatenate(outs, axis=1).astype(BF16)


def _attn_window(qb, kb, vb, slopes, sink):
    b, s, _ = qb.shape
    pad = ((0, 0), (WINDOW, WINDOW), (0, 0))
    kp, vp = jnp.pad(kb, pad), jnp.pad(vb, pad)
    smem = pl.BlockSpec(memory_space=pltpu.SMEM)
    return pl.pallas_call(
        functools.partial(_attn_window_kernel, seq=s),
        grid=(b, s // BLOCK),
        in_specs=[
            smem, smem,
            pl.BlockSpec((1, BLOCK, Q_W), lambda i, j: (i, j, 0)),
            pl.BlockSpec((1, s + 2 * WINDOW, KV_W), lambda i, j: (i, 0, 0)),
            pl.BlockSpec((1, s + 2 * WINDOW, KV_W), lambda i, j: (i, 0, 0)),
        ],
        out_specs=pl.BlockSpec((1, BLOCK, Q_W), lambda i, j: (i, j, 0)),
        out_shape=jax.ShapeDtypeStruct((b, s, Q_W), BF16),
        compiler_params=_cparams(2),
        name="attn_window",
    )(slopes, sink, qb, kp, vp)


def _attn_mem_kernel(q_ref, mkt_ref, mv_ref, o_ref):
    scale = M_HEAD_DIM ** -0.5
    outs = []
    for head in range(M_HEADS):
        sl = slice(head * M_HEAD_DIM, (head + 1) * M_HEAD_DIM)
        logits = _dot(q_ref[0, :, sl], mkt_ref[0, sl, :]) * scale
        m = jnp.max(logits, axis=-1, keepdims=True)
        e = jnp.exp(logits - m)
        p = (e / jnp.sum(e, axis=-1, keepdims=True)).astype(BF16)
        outs.append(_dot(p, mv_ref[0, :, sl]))
    o_ref[0] = jnp.concatenate(outs, axis=1).astype(BF16)


def _attn_mem(qm, mkt, mv, tq):
    b, s, _ = qm.shape
    return pl.pallas_call(
        _attn_mem_kernel,
        grid=(b, s // tq),
        in_specs=[
            pl.BlockSpec((1, tq, M_W), lambda i, j: (i, j, 0)),
            pl.BlockSpec((1, M_W, N_MEM), lambda i, j: (i, 0, 0)),
            pl.BlockSpec((1, N_MEM, M_W), lambda i, j: (i, 0, 0)),
        ],
        out_specs=pl.BlockSpec((1, tq, M_W), lambda i, j: (i, j, 0)),
        out_shape=jax.ShapeDtypeStruct((b, s, M_W), BF16),
        compiler_params=_cparams(2),
        name="attn_mem",
    )(qm, mkt, mv)


def _merge_kernel(x_ref, g_ref, wg_ref, oa_ref, ob_ref, om_ref, wpa_ref, wpb_ref, wpm_ref, wo_ref,
                  x2_ref):
    x = x_ref[...]
    h = _rms(x, g_ref[...]).astype(BF16)
    gates = jax.nn.sigmoid(_dot(h, wg_ref[...]))
    merged = (gates[:, :D_MODEL] * _dot(oa_ref[...], wpa_ref[...])
              + gates[:, D_MODEL:2 * D_MODEL] * _dot(ob_ref[...], wpb_ref[...])
              + gates[:, 2 * D_MODEL:] * _dot(om_ref[...], wpm_ref[...]))
    x2_ref[...] = x + _dot(merged.astype(BF16), wo_ref[...])


def _merge_proj(x, g_attn, w_gate, oa, ob, om, w_pa, w_pb, w_pm, w_o, tm):
    t = x.shape[0]
    row = lambda w: pl.BlockSpec((tm, w), lambda i: (i, 0))
    const = lambda shp: pl.BlockSpec(shp, lambda i: (0, 0))
    return pl.pallas_call(
        _merge_kernel,
        grid=(t // tm,),
        in_specs=[row(D_MODEL), const((1, D_MODEL)), const((D_MODEL, GATE_W)),
                  row(Q_W), row(Q_W), row(M_W),
                  const((Q_W, D_MODEL)), const((Q_W, D_MODEL)), const((M_W, D_MODEL)),
                  const((D_MODEL, D_MODEL))],
        out_specs=row(D_MODEL),
        out_shape=jax.ShapeDtypeStruct((t, D_MODEL), F32),
        compiler_params=_cparams(1),
        name="merge_proj",
    )(x, g_attn.reshape(1, D_MODEL), w_gate, oa, ob, om, w_pa, w_pb, w_pm, w_o)


def _topk_rows(s, aux, k):
    n = s.shape[0]
    iota = lax.broadcasted_iota(I32, s.shape, 0).astype(F32)
    vals, picks = [], []
    for _ in range(k):
        m = jnp.max(s, axis=0, keepdims=True)
        win = jnp.min(jnp.where(s == m, iota, float(n)), axis=0, keepdims=True)
        sel = iota == win
        vals.append(m)
        if aux is None:
            picks.append(win)
        else:
            picks.append(jnp.max(jnp.where(sel, aux, -1.0), axis=0, keepdims=True))
        s = jnp.where(sel, -jnp.inf, s)
    return jnp.concatenate(vals, axis=0), jnp.concatenate(picks, axis=0)


def _pair_candidates(a, b, combine):
    half = PEER_TOPK // 2
    rows = [combine(a[0:1], b)]
    rows += [combine(a[i:i + 1], b[0:half]) for i in range(1, half)]
    rows.append(combine(a[half:], b[0:1]))
    return jnp.concatenate(rows, axis=0)


def _route_kernel(x2_ref, g_ref, wq_ref, k1_ref, k2_ref, h2_ref, idx_ref, gate_ref):
    h2 = _rms(x2_ref[...], g_ref[...])
    h2_ref[...] = h2
    q = _dot(h2.astype(BF16), wq_ref[...]).astype(BF16)
    idx_rows, gate_rows = [], []
    for head in range(PEER_HEADS):
        o = head * 2 * PEER_HALF
        s1 = _dot_nt(k1_ref[head], q[:, o:o + PEER_HALF])
        s2 = _dot_nt(k2_ref[head], q[:, o + PEER_HALF:o + 2 * PEER_HALF])
        sv1, si1 = _topk_rows(s1, None, PEER_TOPK)
        sv2, si2 = _topk_rows(s2, None, PEER_TOPK)
        cand = _pair_candidates(sv1, sv2, lambda a, b: a + b)
        cidx = _pair_candidates(si1, si2, lambda a, b: a * float(PEER_KEYS) + b)
        score, eidx = _topk_rows(cand, cidx, PEER_TOPK)
        e = jnp.exp(score - score[0:1])
        gate_rows.append(e / jnp.sum(e, axis=0, keepdims=True))
        idx_rows.append(eidx)
    idx = (jnp.concatenate(idx_rows, axis=0) * float(PACK_ROWS)).astype(I32)
    idx_ref[...] = idx.T
    gate_ref[...] = jnp.concatenate(gate_rows, axis=0).T


def _peer_route(x2, g_ffn, w_pq, keys1, keys2, tm):
    t = x2.shape[0]
    row = lambda w: pl.BlockSpec((tm, w), lambda i: (i, 0))
    const = lambda shp: pl.BlockSpec(shp, lambda i: (0,) * len(shp))
    kshape = (PEER_HEADS, PEER_KEYS, PEER_HALF)
    return pl.pallas_call(
        _route_kernel,
        grid=(t // tm,),
        in_specs=[row(D_MODEL), const((1, D_MODEL)), const((D_MODEL, PEER_HEADS * 2 * PEER_HALF)),
                  const(kshape), const(kshape)],
        out_specs=[row(D_MODEL), row(PEER_PAIRS), row(PEER_PAIRS)],
        out_shape=[jax.ShapeDtypeStruct((t, D_MODEL), F32),
                   jax.ShapeDtypeStruct((t, PEER_PAIRS), I32),
                   jax.ShapeDtypeStruct((t, PEER_PAIRS), F32)],
        compiler_params=_cparams(1),
        name="peer_route",
    )(x2, g_ffn.reshape(1, D_MODEL), w_pq, keys1, keys2)


def _pack_table(tab):
    e = tab.shape[0]
    bits = lax.bitcast_convert_type(tab.astype(BF16), jnp.uint16).astype(jnp.uint32)
    bits = bits.reshape(e, PACK_ROWS, 2, 128)
    words = bits[:, :, 0, :] | (bits[:, :, 1, :] << 16)
    return lax.bitcast_convert_type(words, I32).reshape(e * PACK_ROWS, 128)


def _table_spec(rows):
    return pl.BlockSpec((rows, 128), lambda i: (0, 0), pipeline_mode=pl.Buffered(1))


ID_GROUP = 8
ID_SLOT = ID_GROUP * PEER_PAIRS


def _for_id_groups(idx_hbm, idx_sm, sem, tm, n_steps, body):
    n_groups = tm // ID_GROUP
    step = pl.program_id(0)
    last_group = n_steps * n_groups - 1
    first = step * n_groups

    def copy(g, slot):
        src = idx_hbm.at[pl.ds(g * ID_SLOT, ID_SLOT)]
        return pltpu.make_async_copy(src, idx_sm[slot], sem.at[slot])

    @pl.when(step == 0)
    def _():
        copy(0, 0).start()

    def group_pair(j, carry):
        g0 = first + 2 * j
        copy(g0 + 1, 1).start()
        copy(g0, 0).wait()
        body(2 * j * ID_GROUP, idx_sm[0])
        copy(jnp.minimum(g0 + 2, last_group), 0).start()
        copy(g0 + 1, 1).wait()
        body((2 * j + 1) * ID_GROUP, idx_sm[1])
        return carry

    lax.fori_loop(0, n_groups // 2, group_pair, 0)

    @pl.when(step == n_steps - 1)
    def _():
        copy(last_group, 0).wait()


def _id_scratch():
    return [pltpu.SMEM((ID_SLOT,), I32), pltpu.SMEM((ID_SLOT,), I32), pltpu.SemaphoreType.DMA((2,))]


def _peer_act_kernel(idx_hbm, h3_ref, gate_ref, tab_ref, coef_ref, part_a, part_b, act_sc,
                     ids_a, ids_b, sem, *, n_steps):
    tm = h3_ref.shape[0]

    def group(t0, ids):
        for s in range(ID_GROUP):
            part_sc = part_b if s % 2 else part_a
            x = h3_ref[t0 + s].astype(BF16)
            for k in range(PEER_PAIRS):
                row = pl.multiple_of(ids[s * PEER_PAIRS + k], PACK_ROWS)
                w = pltpu.bitcast(tab_ref[pl.ds(row, PACK_ROWS), :], BF16)
                prod = (w * x).astype(F32)
                part_sc[k * PACK_ROWS:(k + 1) * PACK_ROWS, :] = prod[0:PACK_ROWS] + prod[PACK_ROWS:]
            part = part_sc[pl.ds(0, PEER_PAIRS, stride=PACK_ROWS), :]
            for r in range(1, PACK_ROWS):
                part = part + part_sc[pl.ds(r, PEER_PAIRS, stride=PACK_ROWS), :]
            act_sc[pl.ds(t0 + s, 1), :] = jnp.sum(part.T, axis=0, keepdims=True)

    _for_id_groups(idx_hbm, (ids_a, ids_b), sem, tm, n_steps, group)
    a = act_sc[...]
    gelu = 0.5 * a * (1.0 + lax.erf(a * (2.0 ** -0.5)))
    coef_ref[...] = gelu * gate_ref[...]


def _peer_act(idx, h3, gate, utab, tm):
    t = h3.shape[0]
    return pl.pallas_call(
        functools.partial(_peer_act_kernel, n_steps=t // tm),
        grid=(t // tm,),
        in_specs=[
            pl.BlockSpec(memory_space=pl.ANY),
            pl.BlockSpec((tm, 8, 128), lambda i: (i, 0, 0)),
            pl.BlockSpec((tm, PEER_PAIRS), lambda i: (i, 0)),
            _table_spec(utab.shape[0]),
        ],
        out_specs=pl.BlockSpec((tm, PEER_PAIRS), lambda i: (i, 0)),
        out_shape=jax.ShapeDtypeStruct((t, PEER_PAIRS), F32),
        scratch_shapes=[pltpu.VMEM((PEER_PAIRS * PACK_ROWS, 128), F32),
                        pltpu.VMEM((PEER_PAIRS * PACK_ROWS, 128), F32),
                        pltpu.VMEM((tm, PEER_PAIRS), F32)] + _id_scratch(),
        compiler_params=_cparams(1),
        name="peer_act",
    )(idx.reshape(-1), h3, gate, utab)


def _peer_out_kernel(idx_hbm, coef_ref, x3_ref, g_ref, tab_ref, y_ref, o_sc, cb_a, cb_b,
                     ids_a, ids_b, sem, *, n_steps):
    tm = x3_ref.shape[0]
    n_acc = 4

    def spread(t, cb_sc):
        c = coef_ref[pl.ds(t, 1), :].astype(BF16).astype(F32)
        bits = lax.bitcast_convert_type(c, I32)
        words = bits | lax.shift_right_logical(bits, 16)
        cb_sc[...] = jnp.broadcast_to(words, (PEER_PAIRS, PEER_PAIRS)).T

    def group(t0, ids):
        spread(t0, cb_a)
        for s in range(ID_GROUP):
            cb_sc = cb_b if s % 2 else cb_a
            if s + 1 < ID_GROUP:
                spread(t0 + s + 1, cb_a if s % 2 else cb_b)
            acc = [jnp.zeros((8, 128), F32) for _ in range(n_acc)]
            for k in range(PEER_PAIRS):
                row = pl.multiple_of(ids[s * PEER_PAIRS + k], PACK_ROWS)
                w = pltpu.bitcast(tab_ref[pl.ds(row, PACK_ROWS), :], BF16)
                c = pltpu.bitcast(jnp.broadcast_to(cb_sc[k:k + 1, :], (PACK_ROWS, 128)), BF16)
                acc[k % n_acc] = acc[k % n_acc] + (c * w).astype(F32)
            o_sc[t0 + s] = (acc[0] + acc[1]) + (acc[2] + acc[3])

    _for_id_groups(idx_hbm, (ids_a, ids_b), sem, tm, n_steps, group)
    z = x3_ref[...] + o_sc[...]
    ss = jnp.sum(jnp.sum(z * z, axis=2, keepdims=True), axis=1, keepdims=True)
    y_ref[...] = (z * lax.rsqrt(ss * (1.0 / D_MODEL) + EPS)) * g_ref[...]


def _peer_out(idx, coef, x3, g_final, vtab, tm):
    t = x3.shape[0]
    return pl.pallas_call(
        functools.partial(_peer_out_kernel, n_steps=t // tm),
        grid=(t // tm,),
        in_specs=[
            pl.BlockSpec(memory_space=pl.ANY),
            pl.BlockSpec((tm, PEER_PAIRS), lambda i: (i, 0)),
            pl.BlockSpec((tm, 8, 128), lambda i: (i, 0, 0)),
            pl.BlockSpec((1, 8, 128), lambda i: (0, 0, 0)),
            _table_spec(vtab.shape[0]),
        ],
        out_specs=pl.BlockSpec((tm, 8, 128), lambda i: (i, 0, 0)),
        out_shape=jax.ShapeDtypeStruct((t, 8, 128), F32),
        scratch_shapes=[pltpu.VMEM((tm, 8, 128), F32), pltpu.VMEM((PEER_PAIRS, PEER_PAIRS), I32),
                        pltpu.VMEM((PEER_PAIRS, PEER_PAIRS), I32)] + _id_scratch(),
        compiler_params=_cparams(1),
        name="peer_out",
    )(idx.reshape(-1), coef, x3, g_final.reshape(1, 8, 128), vtab)


def _pick(n, prefs):
    for p in prefs:
        if n % p == 0:
            return p
    raise ValueError(f"no tile in {prefs} divides {n}")


def _trunk(x, mem, w, g_final):
    b, s, _ = x.shape
    t = b * s
    mkt, mv = _mem_kv(mem, w["g_mem"], w["w_mkv"])
    qa, kat, va, qb, kb, vb, qm = _qkv_proj(x, w["g_attn"], w["w_qkv"], w["g_qa"], w["g_ka"],
                                           _pick(s, (512, 256, 128)))
    oa = _attn_global(qa, kat, va, _pick(s, (256, 128)), _pick(s, (1024, 512, 256, 128)))
    ob = _attn_window(qb, kb, vb, w["slopes"], w["sink_b"])
    om = _attn_mem(qm, mkt, mv, _pick(s, (512, 256, 128)))
    flat = lambda a: a.reshape(t, a.shape[-1])
    x2 = _merge_proj(flat(x), w["g_attn"], w["w_gate"], flat(oa), flat(ob), flat(om),
                     w["w_pa"], w["w_pb"], w["w_pm"], w["w_o"], _pick(t, (256, 128)))
    h2, idx, gate = _peer_route(x2, w["g_ffn"], w["w_pq"], w["keys1"], w["keys2"],
                                _pick(t, (256, 128)))
    tm = _pick(t, (64, 32, 16, 8))
    coef = _peer_act(idx, h2.reshape(t, 8, 128), gate, w["utab"], tm)
    y3 = _peer_out(idx, coef, x2.reshape(t, 8, 128), g_final, w["vtab"], tm)
    return y3.reshape(b, s, D_MODEL)


def kernel(x_prompt, x_sample, mem_prompt, mem_sample, g_attn, w_in, g_qa, g_ka, sink_b, w_mkv,
           g_mem, w_pa, w_pb, w_pm, w_o, g_ffn, w_pq, peer_keys1, peer_keys2, peer_u, peer_v,
           g_final):
    assert g_attn.shape[0] == 1, "single-layer trunk"
    w_in_b = w_in[0].astype(BF16)
    w = dict(
        g_attn=g_attn[0], w_qkv=w_in_b[:, :QKV_W], w_gate=w_in_b[:, QKV_W:],
        g_qa=g_qa[0], g_ka=g_ka[0], sink_b=sink_b[0], w_mkv=w_mkv[0], g_mem=g_mem[0],
        w_pa=w_pa[0].astype(BF16), w_pb=w_pb[0].astype(BF16), w_pm=w_pm[0].astype(BF16),
        w_o=w_o[0].astype(BF16), g_ffn=g_ffn[0], w_pq=w_pq[0].astype(BF16),
        keys1=peer_keys1[0].astype(BF16), keys2=peer_keys2[0].astype(BF16),
        utab=_pack_table(peer_u[0]), vtab=_pack_table(peer_v[0]),
        slopes=2.0 ** (-8.0 * jnp.arange(1, Q_HEADS + 1, dtype=F32) / Q_HEADS),
    )
    return (_trunk(x_prompt, mem_prompt, w, g_final), _trunk(x_sample, mem_sample, w, g_final))
```

```python
import functools

import jax
import jax.numpy as jnp
import numpy as np
from jax import lax
from jax.experimental import pallas as pl
from jax.experimental.pallas import tpu as pltpu

F32 = jnp.float32
BF16 = jnp.bfloat16
I32 = jnp.int32

D_MODEL = 1024
HEAD_DIM = 64
Q_HEADS = 8
KV_HEADS = 2
GROUP = Q_HEADS // KV_HEADS
Q_W = Q_HEADS * HEAD_DIM
KV_W = KV_HEADS * HEAD_DIM
M_HEADS = 4
M_HEAD_DIM = 128
M_W = M_HEADS * M_HEAD_DIM
N_MEM = 256
QKV_W = 2 * (Q_W + 2 * KV_W) + M_W
GATE_W = 3 * D_MODEL
GRID_W = 64
WINDOW = 128
BLOCK = 128
SPAN = BLOCK + 2 * WINDOW
ROPE_THETA = 10000.0
ROPE_HALF = HEAD_DIM // 4
PEER_HEADS = 8
PEER_KEYS = 128
PEER_HALF = 128
PEER_TOPK = 16
PEER_PAIRS = PEER_HEADS * PEER_TOPK
EPS = 1e-6
NEG_INF = -1e30

VMEM_LIMIT_BYTES = 56 * 1024 * 1024
PACK_ROWS = 4
SAFE_LOGIT_BOUND = 40.0


def _cparams(n_axes):
    return pltpu.CompilerParams(
        dimension_semantics=("arbitrary",) * n_axes,
        vmem_limit_bytes=VMEM_LIMIT_BYTES,
    )


def _rms(x, g):
    ms = jnp.mean(x * x, axis=-1, keepdims=True)
    return (x * lax.rsqrt(ms + EPS)) * g


def _dot(a, b):
    return jnp.dot(a, b, preferred_element_type=F32)


def _dot_nt(a, b):
    return lax.dot_general(a, b, (((1,), (1,)), ((), ())), preferred_element_type=F32)


def _dot_f32_by_01(a, m01):
    hi = a.astype(BF16)
    r1 = a - hi.astype(F32)
    mid = r1.astype(BF16)
    lo = (r1 - mid.astype(F32)).astype(BF16)
    return _dot(lo, m01) + _dot(mid, m01) + _dot(hi, m01)


def _mem_kv_kernel(mem_ref, g_ref, w_ref, mkt_ref, mv_ref):
    h = _rms(mem_ref[0], g_ref[...]).astype(BF16)
    z = _dot(h, w_ref[...])
    mkt_ref[0] = z[:, :M_W].T.astype(BF16)
    mv_ref[0] = z[:, M_W:].astype(BF16)


def _mem_kv(mem, g_mem, w_mkv):
    b = mem.shape[0]
    return pl.pallas_call(
        _mem_kv_kernel,
        grid=(b,),
        in_specs=[
            pl.BlockSpec((1, N_MEM, D_MODEL), lambda i: (i, 0, 0)),
            pl.BlockSpec((1, D_MODEL), lambda i: (0, 0)),
            pl.BlockSpec((D_MODEL, 2 * M_W), lambda i: (0, 0)),
        ],
        out_specs=[
            pl.BlockSpec((1, M_W, N_MEM), lambda i: (i, 0, 0)),
            pl.BlockSpec((1, N_MEM, M_W), lambda i: (i, 0, 0)),
        ],
        out_shape=[
            jax.ShapeDtypeStruct((b, M_W, N_MEM), BF16),
            jax.ShapeDtypeStruct((b, N_MEM, M_W), BF16),
        ],
        compiler_params=_cparams(1),
        name="mem_kv",
    )(mem, g_mem.reshape(1, D_MODEL), w_mkv.astype(BF16))


def _head_norm_rope(q, g, bd, c, s_up, s_dn, width):
    ss = _dot_f32_by_01(q * q, bd)
    qn = (q * lax.rsqrt(ss * (1.0 / HEAD_DIM) + EPS)) * g
    up = pltpu.roll(qn, width - ROPE_HALF, 1)
    dn = pltpu.roll(qn, ROPE_HALF, 1)
    return qn * c + up * s_up + dn * s_dn


def _qkv_kernel(x_ref, g_ref, w_ref, gq_ref, gk_ref, c_ref, su_ref, sd_ref, bdq_ref, bdk_ref,
                qa_ref, kat_ref, va_ref, qb_ref, kb_ref, vb_ref, qm_ref):
    h = _rms(x_ref[0], g_ref[...]).astype(BF16)
    z = _dot(h, w_ref[...])
    c2, su2, sd2 = c_ref[...], su_ref[...], sd_ref[...]
    rep = Q_W // KV_W
    c8 = jnp.concatenate([c2] * rep, axis=1)
    su8 = jnp.concatenate([su2] * rep, axis=1)
    sd8 = jnp.concatenate([sd2] * rep, axis=1)
    o = 0
    qa = _head_norm_rope(z[:, o:o + Q_W], gq_ref[...], bdq_ref[...], c8, su8, sd8, Q_W)
    qa_ref[0] = (qa * (HEAD_DIM ** -0.5)).astype(BF16)
    o += Q_W
    ka = _head_norm_rope(z[:, o:o + KV_W], gk_ref[...], bdk_ref[...], c2, su2, sd2, KV_W)
    kat_ref[0] = ka.T.astype(BF16)
    o += KV_W
    va_ref[0] = z[:, o:o + KV_W].astype(BF16)
    o += KV_W
    qb_ref[0] = (z[:, o:o + Q_W] * (HEAD_DIM ** -0.5)).astype(BF16)
    o += Q_W
    kb_ref[0] = z[:, o:o + KV_W].astype(BF16)
    o += KV_W
    vb_ref[0] = z[:, o:o + KV_W].astype(BF16)
    o += KV_W
    qm_ref[0] = z[:, o:o + M_W].astype(BF16)


def _rope_tables(seq_len):
    rows = seq_len // GRID_W
    row_ids = jnp.repeat(jnp.arange(rows, dtype=F32), GRID_W)
    col_ids = jnp.tile(jnp.arange(GRID_W, dtype=F32), rows)
    inv_freq = ROPE_THETA ** (-jnp.arange(ROPE_HALF, dtype=F32) / ROPE_HALF)
    ang_r = row_ids[:, None] * inv_freq[None, :]
    ang_c = col_ids[:, None] * inv_freq[None, :]
    cr, sr, cc, sc = jnp.cos(ang_r), jnp.sin(ang_r), jnp.cos(ang_c), jnp.sin(ang_c)
    z = jnp.zeros_like(sr)
    c = jnp.concatenate([cr, cr, cc, cc], axis=1)
    s_up = jnp.concatenate([-sr, z, -sc, z], axis=1)
    s_dn = jnp.concatenate([z, sr, z, sc], axis=1)
    two = lambda t: jnp.concatenate([t, t], axis=1)
    return two(c), two(s_up), two(s_dn)


def _block_diag_ones(width):
    i = np.arange(width) // HEAD_DIM
    return jnp.asarray((i[:, None] == i[None, :]).astype(np.float32), dtype=BF16)


def _qkv_proj(x, g_attn, w_qkv, g_qa, g_ka, tm):
    b, s, _ = x.shape
    c, su, sd = _rope_tables(s)
    gq = jnp.tile(g_qa, Q_HEADS).reshape(1, Q_W)
    gk = jnp.tile(g_ka, KV_HEADS).reshape(1, KV_W)
    row = lambda w: pl.BlockSpec((1, tm, w), lambda i, j: (i, j, 0))
    tab = pl.BlockSpec((tm, KV_W), lambda i, j: (j, 0))
    const = lambda shp: pl.BlockSpec(shp, lambda i, j: (0,) * len(shp))
    sds = lambda w: jax.ShapeDtypeStruct((b, s, w), BF16)
    return pl.pallas_call(
        _qkv_kernel,
        grid=(b, s // tm),
        in_specs=[row(D_MODEL), const((1, D_MODEL)), const((D_MODEL, QKV_W)), const((1, Q_W)),
                  const((1, KV_W)), tab, tab, tab, const((Q_W, Q_W)), const((KV_W, KV_W))],
        out_specs=[row(Q_W), pl.BlockSpec((1, KV_W, tm), lambda i, j: (i, 0, j)), row(KV_W),
                   row(Q_W), row(KV_W), row(KV_W), row(M_W)],
        out_shape=[sds(Q_W), jax.ShapeDtypeStruct((b, KV_W, s), BF16), sds(KV_W),
                   sds(Q_W), sds(KV_W), sds(KV_W), sds(M_W)],
        compiler_params=_cparams(2),
        name="qkv_proj",
    )(x, g_attn.reshape(1, D_MODEL), w_qkv, gq, gk, c, su, sd,
      _block_diag_ones(Q_W), _block_diag_ones(KV_W))


def _kv_lane_mask(kvh, dtype):
    lane = lax.broadcasted_iota(I32, (1, KV_W), 1)
    return ((lane // HEAD_DIM) == kvh).astype(dtype)


def _attn_global_kernel(q_ref, kt_ref, v_ref, o_ref, q4_sc, m_sc, l_sc, acc_sc, k2_sm, *, seq, tk):
    tq = q_ref.shape[1]
    rows = GROUP * tq
    n_slab = tk // 128

    @pl.when(pl.program_id(1) == 0)
    def _():
        def key_norms(ci, best):
            off = pl.multiple_of(ci * tk, tk)
            kc = kt_ref[0, :, pl.ds(off, tk)].astype(F32)
            kk = kc * kc
            return tuple(jnp.maximum(best[h], jnp.sum(kk[h * HEAD_DIM:(h + 1) * HEAD_DIM],
                                                      axis=0, keepdims=True))
                         for h in range(KV_HEADS))

        zero = jnp.zeros((1, tk), F32)
        best = lax.fori_loop(0, seq // tk, key_norms, (zero,) * KV_HEADS)
        for h in range(KV_HEADS):
            k2_sm[h] = jnp.max(best[h], axis=1, keepdims=True)[0, 0]

    outs = [None] * Q_HEADS
    for kvh in range(KV_HEADS):
        for gi in range(GROUP):
            head = kvh * GROUP + gi
            qh = q_ref[0, :, head * HEAD_DIM:(head + 1) * HEAD_DIM]
            zeros = jnp.zeros_like(qh)
            q4_sc[gi * tq:(gi + 1) * tq, :] = jnp.concatenate(
                [qh, zeros] if kvh == 0 else [zeros, qh], axis=1)
        q4 = q4_sc[...].astype(F32)
        q2 = jnp.max(jnp.sum(q4 * q4, axis=1, keepdims=True), axis=0, keepdims=True)[0, 0]
        bound = jnp.sqrt(q2 * k2_sm[kvh])

        @pl.when(bound <= SAFE_LOGIT_BOUND)
        def _():
            m_sc[...] = jnp.full((rows, 128), bound, F32)

        @pl.when(bound > SAFE_LOGIT_BOUND)
        def _():
            m_sc[...] = jnp.full((rows, 128), -jnp.inf, F32)

            def row_max(ci, carry):
                off = pl.multiple_of(ci * tk, tk)
                s = _dot(q4_sc[...], kt_ref[0, :, pl.ds(off, tk)])
                m = m_sc[...]
                for j in range(n_slab):
                    m = jnp.maximum(m, s[:, j * 128:(j + 1) * 128])
                m_sc[...] = m
                return carry

            lax.fori_loop(0, seq // tk, row_max, 0)
            m_sc[...] = jnp.broadcast_to(jnp.max(m_sc[...], axis=-1, keepdims=True), (rows, 128))

        l_sc[...] = jnp.zeros((rows, 128), F32)
        acc_sc[...] = jnp.zeros((rows, KV_W), F32)

        def accumulate(ci, carry):
            off = pl.multiple_of(ci * tk, tk)
            s = _dot(q4_sc[...], kt_ref[0, :, pl.ds(off, tk)])
            m = m_sc[...]
            l = l_sc[...]
            ps = []
            for j in range(n_slab):
                pj = jnp.exp(s[:, j * 128:(j + 1) * 128] - m)
                l = l + pj
                ps.append(pj.astype(BF16))
            l_sc[...] = l
            acc_sc[...] += _dot(jnp.concatenate(ps, axis=1), v_ref[0, pl.ds(off, tk), :])
            return carry

        lax.fori_loop(0, seq // tk, accumulate, 0)
        o = acc_sc[...] / jnp.sum(l_sc[...], axis=-1, keepdims=True)
        for gi in range(GROUP):
            outs[kvh * GROUP + gi] = o[gi * tq:(gi + 1) * tq, kvh * HEAD_DIM:(kvh + 1) * HEAD_DIM]
    o_ref[0] = jnp.concatenate(outs, axis=1).astype(BF16)


def _attn_global(qa, kat, va, tq, tk):
    b, s, _ = qa.shape
    return pl.pallas_call(
        functools.partial(_attn_global_kernel, seq=s, tk=tk),
        grid=(b, s // tq),
        in_specs=[
            pl.BlockSpec((1, tq, Q_W), lambda i, j: (i, j, 0)),
            pl.BlockSpec((1, KV_W, s), lambda i, j: (i, 0, 0)),
            pl.BlockSpec((1, s, KV_W), lambda i, j: (i, 0, 0)),
        ],
        out_specs=pl.BlockSpec((1, tq, Q_W), lambda i, j: (i, j, 0)),
        out_shape=jax.ShapeDtypeStruct((b, s, Q_W), BF16),
        scratch_shapes=[pltpu.VMEM((GROUP * tq, KV_W), BF16), pltpu.VMEM((GROUP * tq, 128), F32),
                        pltpu.VMEM((GROUP * tq, 128), F32), pltpu.VMEM((GROUP * tq, KV_W), F32),
                        pltpu.SMEM((KV_HEADS,), F32)],
        compiler_params=_cparams(2),
        name="attn_global",
    )(qa, kat, va)


def _attn_window_kernel(slope_ref, sink_ref, q_ref, k_ref, v_ref, o_ref, *, seq):
    i = pl.program_id(1)
    start = pl.multiple_of(i * BLOCK, BLOCK)
    ks = k_ref[0, pl.ds(start, SPAN), :]
    vs = v_ref[0, pl.ds(start, SPAN), :]
    r = lax.broadcasted_iota(I32, (BLOCK, SPAN), 0)
    c = lax.broadcasted_iota(I32, (BLOCK, SPAN), 1)
    dist = r + WINDOW - c
    adist = jnp.abs(dist)
    key_pos = start - WINDOW + c
    valid = (adist <= WINDOW) & (key_pos >= 0) & (key_pos < seq)
    adist_f = adist.astype(F32)
    outs = [None] * Q_HEADS
    for kvh in range(KV_HEADS):
        heads = range(kvh * GROUP, (kvh + 1) * GROUP)
        qz = []
        for head in heads:
            qh = q_ref[0, :, head * HEAD_DIM:(head + 1) * HEAD_DIM]
            zeros = jnp.zeros_like(qh)
            qz.append(jnp.concatenate([qh, zeros] if kvh == 0 else [zeros, qh], axis=1))
        s = _dot_nt(jnp.concatenate(qz, axis=0), ks)
        logits = jnp.concatenate(
            [jnp.where(valid, s[g * BLOCK:(g + 1) * BLOCK] - slope_ref[head] * adist_f, NEG_INF)
             for g, head in enumerate(heads)], axis=0)
        sink = jnp.concatenate([jnp.full((BLOCK, 1), sink_ref[head], F32) for head in heads], axis=0)
        m = jnp.maximum(jnp.max(logits, axis=-1, keepdims=True), sink)
        e = jnp.exp(logits - m)
        denom = jnp.sum(e, axis=-1, keepdims=True) + jnp.exp(sink - m)
        o = _dot((e / denom).astype(BF16), vs)
        for g, head in enumerate(heads):
            outs[head] = o[g * BLOCK:(g + 1) * BLOCK, kvh * HEAD_DIM:(kvh + 1) * HEAD_DIM]
    o_ref[0] = jnp.concatenate(outs, axis=1).astype(BF16)


def _attn_window(qb, kb, vb, slopes, sink):
    b, s, _ = qb.shape
    pad = ((0, 0), (WINDOW, WINDOW), (0, 0))
    kp, vp = jnp.pad(kb, pad), jnp.pad(vb, pad)
    smem = pl.BlockSpec(memory_space=pltpu.SMEM)
    return pl.pallas_call(
        functools.partial(_attn_window_kernel, seq=s),
        grid=(b, s // BLOCK),
        in_specs=[
            smem, smem,
            pl.BlockSpec((1, BLOCK, Q_W), lambda i, j: (i, j, 0)),
            pl.BlockSpec((1, s + 2 * WINDOW, KV_W), lambda i, j: (i, 0, 0)),
            pl.BlockSpec((1, s + 2 * WINDOW, KV_W), lambda i, j: (i, 0, 0)),
        ],
        out_specs=pl.BlockSpec((1, BLOCK, Q_W), lambda i, j: (i, j, 0)),
        out_shape=jax.ShapeDtypeStruct((b, s, Q_W), BF16),
        compiler_params=_cparams(2),
        name="attn_window",
    )(slopes, sink, qb, kp, vp)


def _attn_mem_kernel(q_ref, mkt_ref, mv_ref, o_ref):
    scale = M_HEAD_DIM ** -0.5
    outs = []
    for head in range(M_HEADS):
        sl = slice(head * M_HEAD_DIM, (head + 1) * M_HEAD_DIM)
        logits = _dot(q_ref[0, :, sl], mkt_ref[0, sl, :]) * scale
        m = jnp.max(logits, axis=-1, keepdims=True)
        e = jnp.exp(logits - m)
        p = (e / jnp.sum(e, axis=-1, keepdims=True)).astype(BF16)
        outs.append(_dot(p, mv_ref[0, :, sl]))
    o_ref[0] = jnp.concatenate(outs, axis=1).astype(BF16)


def _attn_mem(qm, mkt, mv, tq):
    b, s, _ = qm.shape
    return pl.pallas_call(
        _attn_mem_kernel,
        grid=(b, s // tq),
        in_specs=[
            pl.BlockSpec((1, tq, M_W), lambda i, j: (i, j, 0)),
            pl.BlockSpec((1, M_W, N_MEM), lambda i, j: (i, 0, 0)),
            pl.BlockSpec((1, N_MEM, M_W), lambda i, j: (i, 0, 0)),
        ],
        out_specs=pl.BlockSpec((1, tq, M_W), lambda i, j: (i, j, 0)),
        out_shape=jax.ShapeDtypeStruct((b, s, M_W), BF16),
        compiler_params=_cparams(2),
        name="attn_mem",
    )(qm, mkt, mv)


def _merge_kernel(x_ref, g_ref, wg_ref, oa_ref, ob_ref, om_ref, wpa_ref, wpb_ref, wpm_ref, wo_ref,
                  x2_ref):
    x = x_ref[...]
    h = _rms(x, g_ref[...]).astype(BF16)
    gates = jax.nn.sigmoid(_dot(h, wg_ref[...]))
    merged = (gates[:, :D_MODEL] * _dot(oa_ref[...], wpa_ref[...])
              + gates[:, D_MODEL:2 * D_MODEL] * _dot(ob_ref[...], wpb_ref[...])
              + gates[:, 2 * D_MODEL:] * _dot(om_ref[...], wpm_ref[...]))
    x2_ref[...] = x + _dot(merged.astype(BF16), wo_ref[...])


def _merge_proj(x, g_attn, w_gate, oa, ob, om, w_pa, w_pb, w_pm, w_o, tm):
    t = x.shape[0]
    row = lambda w: pl.BlockSpec((tm, w), lambda i: (i, 0))
    const = lambda shp: pl.BlockSpec(shp, lambda i: (0, 0))
    return pl.pallas_call(
        _merge_kernel,
        grid=(t // tm,),
        in_specs=[row(D_MODEL), const((1, D_MODEL)), const((D_MODEL, GATE_W)),
                  row(Q_W), row(Q_W), row(M_W),
                  const((Q_W, D_MODEL)), const((Q_W, D_MODEL)), const((M_W, D_MODEL)),
                  const((D_MODEL, D_MODEL))],
        out_specs=row(D_MODEL),
        out_shape=jax.ShapeDtypeStruct((t, D_MODEL), F32),
        compiler_params=_cparams(1),
        name="merge_proj",
    )(x, g_attn.reshape(1, D_MODEL), w_gate, oa, ob, om, w_pa, w_pb, w_pm, w_o)


def _topk_rows(s, aux, k):
    n = s.shape[0]
    iota = lax.broadcasted_iota(I32, s.shape, 0).astype(F32)
    vals, picks = [], []
    for _ in range(k):
        m = jnp.max(s, axis=0, keepdims=True)
        win = jnp.min(jnp.where(s == m, iota, float(n)), axis=0, keepdims=True)
        sel = iota == win
        vals.append(m)
        if aux is None:
            picks.append(win)
        else:
            picks.append(jnp.max(jnp.where(sel, aux, -1.0), axis=0, keepdims=True))
        s = jnp.where(sel, -jnp.inf, s)
    return jnp.concatenate(vals, axis=0), jnp.concatenate(picks, axis=0)


def _pair_candidates(a, b, combine):
    half = PEER_TOPK // 2
    rows = [combine(a[0:1], b)]
    rows += [combine(a[i:i + 1], b[0:half]) for i in range(1, half)]
    rows.append(combine(a[half:], b[0:1]))
    return jnp.concatenate(rows, axis=0)


def _route_kernel(x2_ref, g_ref, wq_ref, k1_ref, k2_ref, h2_ref, idx_ref, gate_ref):
    h2 = _rms(x2_ref[...], g_ref[...])
    h2_ref[...] = h2
    q = _dot(h2.astype(BF16), wq_ref[...]).astype(BF16)
    idx_rows, gate_rows = [], []
    for head in range(PEER_HEADS):
        o = head * 2 * PEER_HALF
        s1 = _dot_nt(k1_ref[head], q[:, o:o + PEER_HALF])
        s2 = _dot_nt(k2_ref[head], q[:, o + PEER_HALF:o + 2 * PEER_HALF])
        sv1, si1 = _topk_rows(s1, None, PEER_TOPK)
        sv2, si2 = _topk_rows(s2, None, PEER_TOPK)
        cand = _pair_candidates(sv1, sv2, lambda a, b: a + b)
        cidx = _pair_candidates(si1, si2, lambda a, b: a * float(PEER_KEYS) + b)
        score, eidx = _topk_rows(cand, cidx, PEER_TOPK)
        e = jnp.exp(score - score[0:1])
        gate_rows.append(e / jnp.sum(e, axis=0, keepdims=True))
        idx_rows.append(eidx)
    idx = (jnp.concatenate(idx_rows, axis=0) * float(PACK_ROWS)).astype(I32)
    idx_ref[...] = idx.T
    gate_ref[...] = jnp.concatenate(gate_rows, axis=0).T


def _peer_route(x2, g_ffn, w_pq, keys1, keys2, tm):
    t = x2.shape[0]
    row = lambda w: pl.BlockSpec((tm, w), lambda i: (i, 0))
    const = lambda shp: pl.BlockSpec(shp, lambda i: (0,) * len(shp))
    kshape = (PEER_HEADS, PEER_KEYS, PEER_HALF)
    return pl.pallas_call(
        _route_kernel,
        grid=(t // tm,),
        in_specs=[row(D_MODEL), const((1, D_MODEL)), const((D_MODEL, PEER_HEADS * 2 * PEER_HALF)),
                  const(kshape), const(kshape)],
        out_specs=[row(D_MODEL), row(PEER_PAIRS), row(PEER_PAIRS)],
        out_shape=[jax.ShapeDtypeStruct((t, D_MODEL), F32),
                   jax.ShapeDtypeStruct((t, PEER_PAIRS), I32),
                   jax.ShapeDtypeStruct((t, PEER_PAIRS), F32)],
        compiler_params=_cparams(1),
        name="peer_route",
    )(x2, g_ffn.reshape(1, D_MODEL), w_pq, keys1, keys2)


def _pack_table(tab):
    e = tab.shape[0]
    bits = lax.bitcast_convert_type(tab.astype(BF16), jnp.uint16).astype(jnp.uint32)
    bits = bits.reshape(e, PACK_ROWS, 2, 128)
    words = bits[:, :, 0, :] | (bits[:, :, 1, :] << 16)
    return lax.bitcast_convert_type(words, I32).reshape(e * PACK_ROWS, 128)


def _table_spec(rows):
    return pl.BlockSpec((rows, 128), lambda i: (0, 0), pipeline_mode=pl.Buffered(1))


ID_GROUP = 8
ID_SLOT = ID_GROUP * PEER_PAIRS


def _for_id_groups(idx_hbm, idx_sm, sem, tm, n_steps, body):
    n_groups = tm // ID_GROUP
    step = pl.program_id(0)
    last_group = n_steps * n_groups - 1
    first = step * n_groups

    def copy(g, slot):
        src = idx_hbm.at[pl.ds(g * ID_SLOT, ID_SLOT)]
        return pltpu.make_async_copy(src, idx_sm[slot], sem.at[slot])

    @pl.when(step == 0)
    def _():
        copy(0, 0).start()

    def group_pair(j, carry):
        g0 = first + 2 * j
        copy(g0 + 1, 1).start()
        copy(g0, 0).wait()
        body(2 * j * ID_GROUP, idx_sm[0])
        copy(jnp.minimum(g0 + 2, last_group), 0).start()
        copy(g0 + 1, 1).wait()
        body((2 * j + 1) * ID_GROUP, idx_sm[1])
        return carry

    lax.fori_loop(0, n_groups // 2, group_pair, 0)

    @pl.when(step == n_steps - 1)
    def _():
        copy(last_group, 0).wait()


def _id_scratch():
    return [pltpu.SMEM((ID_SLOT,), I32), pltpu.SMEM((ID_SLOT,), I32), pltpu.SemaphoreType.DMA((2,))]


def _peer_act_kernel(idx_hbm, h3_ref, gate_ref, tab_ref, coef_ref, part_a, part_b, act_sc,
                     ids_a, ids_b, sem, *, n_steps):
    tm = h3_ref.shape[0]

    def group(t0, ids):
        for s in range(ID_GROUP):
            part_sc = part_b if s % 2 else part_a
            x = h3_ref[t0 + s].astype(BF16)
            for k in range(PEER_PAIRS):
                row = pl.multiple_of(ids[s * PEER_PAIRS + k], PACK_ROWS)
                w = pltpu.bitcast(tab_ref[pl.ds(row, PACK_ROWS), :], BF16)
                prod = (w * x).astype(F32)
                part_sc[k * PACK_ROWS:(k + 1) * PACK_ROWS, :] = prod[0:PACK_ROWS] + prod[PACK_ROWS:]
            part = part_sc[pl.ds(0, PEER_PAIRS, stride=PACK_ROWS), :]
            for r in range(1, PACK_ROWS):
                part = part + part_sc[pl.ds(r, PEER_PAIRS, stride=PACK_ROWS), :]
            act_sc[pl.ds(t0 + s, 1), :] = jnp.sum(part.T, axis=0, keepdims=True)

    _for_id_groups(idx_hbm, (ids_a, ids_b), sem, tm, n_steps, group)
    a = act_sc[...]
    gelu = 0.5 * a * (1.0 + lax.erf(a * (2.0 ** -0.5)))
    coef_ref[...] = gelu * gate_ref[...]


def _peer_act(idx, h3, gate, utab, tm):
    t = h3.shape[0]
    return pl.pallas_call(
        functools.partial(_peer_act_kernel, n_steps=t // tm),
        grid=(t // tm,),
        in_specs=[
            pl.BlockSpec(memory_space=pl.ANY),
            pl.BlockSpec((tm, 8, 128), lambda i: (i, 0, 0)),
            pl.BlockSpec((tm, PEER_PAIRS), lambda i: (i, 0)),
            _table_spec(utab.shape[0]),
        ],
        out_specs=pl.BlockSpec((tm, PEER_PAIRS), lambda i: (i, 0)),
        out_shape=jax.ShapeDtypeStruct((t, PEER_PAIRS), F32),
        scratch_shapes=[pltpu.VMEM((PEER_PAIRS * PACK_ROWS, 128), F32),
                        pltpu.VMEM((PEER_PAIRS * PACK_ROWS, 128), F32),
                        pltpu.VMEM((tm, PEER_PAIRS), F32)] + _id_scratch(),
        compiler_params=_cparams(1),
        name="peer_act",
    )(idx.reshape(-1), h3, gate, utab)


def _peer_out_kernel(idx_hbm, coef_ref, x3_ref, g_ref, tab_ref, y_ref, o_sc, cb_a, cb_b,
                     ids_a, ids_b, sem, *, n_steps):
    tm = x3_ref.shape[0]
    n_acc = 4

    def spread(t, cb_sc):
        c = coef_ref[pl.ds(t, 1), :].astype(BF16).astype(F32)
        bits = lax.bitcast_convert_type(c, I32)
        words = bits | lax.shift_right_logical(bits, 16)
        cb_sc[...] = jnp.broadcast_to(words, (PEER_PAIRS, PEER_PAIRS)).T

    def group(t0, ids):
        spread(t0, cb_a)
        for s in range(ID_GROUP):
            cb_sc = cb_b if s % 2 else cb_a
            if s + 1 < ID_GROUP:
                spread(t0 + s + 1, cb_a if s % 2 else cb_b)
            acc = [jnp.zeros((8, 128), F32) for _ in range(n_acc)]
            for k in range(PEER_PAIRS):
                row = pl.multiple_of(ids[s * PEER_PAIRS + k], PACK_ROWS)
                w = pltpu.bitcast(tab_ref[pl.ds(row, PACK_ROWS), :], BF16)
                c = pltpu.bitcast(jnp.broadcast_to(cb_sc[k:k + 1, :], (PACK_ROWS, 128)), BF16)
                acc[k % n_acc] = acc[k % n_acc] + (c * w).astype(F32)
            o_sc[t0 + s] = (acc[0] + acc[1]) + (acc[2] + acc[3])

    _for_id_groups(idx_hbm, (ids_a, ids_b), sem, tm, n_steps, group)
    z = x3_ref[...] + o_sc[...]
    ss = jnp.sum(jnp.sum(z * z, axis=2, keepdims=True), axis=1, keepdims=True)
    y_ref[...] = (z * lax.rsqrt(ss * (1.0 / D_MODEL) + EPS)) * g_ref[...]


def _peer_out(idx, coef, x3, g_final, vtab, tm):
    t = x3.shape[0]
    return pl.pallas_call(
        functools.partial(_peer_out_kernel, n_steps=t // tm),
        grid=(t // tm,),
        in_specs=[
            pl.BlockSpec(memory_space=pl.ANY),
            pl.BlockSpec((tm, PEER_PAIRS), lambda i: (i, 0)),
            pl.BlockSpec((tm, 8, 128), lambda i: (i, 0, 0)),
            pl.BlockSpec((1, 8, 128), lambda i: (0, 0, 0)),
            _table_spec(vtab.shape[0]),
        ],
        out_specs=pl.BlockSpec((tm, 8, 128), lambda i: (i, 0, 0)),
        out_shape=jax.ShapeDtypeStruct((t, 8, 128), F32),
        scratch_shapes=[pltpu.VMEM((tm, 8, 128), F32), pltpu.VMEM((PEER_PAIRS, PEER_PAIRS), I32),
                        pltpu.VMEM((PEER_PAIRS, PEER_PAIRS), I32)] + _id_scratch(),
        compiler_params=_cparams(1),
        name="peer_out",
    )(idx.reshape(-1), coef, x3, g_final.reshape(1, 8, 128), vtab)


def _pick(n, prefs):
    for p in prefs:
        if n % p == 0:
            return p
    raise ValueError(f"no tile in {prefs} divides {n}")


def _trunk(x, mem, w, g_final):
    b, s, _ = x.shape
    t = b * s
    mkt, mv = _mem_kv(mem, w["g_mem"], w["w_mkv"])
    qa, kat, va, qb, kb, vb, qm = _qkv_proj(x, w["g_attn"], w["w_qkv"], w["g_qa"], w["g_ka"],
                                           _pick(s, (512, 256, 128)))
    oa = _attn_global(qa, kat, va, _pick(s, (512, 256, 128)), _pick(s, (1024, 512, 256, 128)))
    ob = _attn_window(qb, kb, vb, w["slopes"], w["sink_b"])
    om = _attn_mem(qm, mkt, mv, _pick(s, (512, 256, 128)))
    flat = lambda a: a.reshape(t, a.shape[-1])
    x2 = _merge_proj(flat(x), w["g_attn"], w["w_gate"], flat(oa), flat(ob), flat(om),
                     w["w_pa"], w["w_pb"], w["w_pm"], w["w_o"], _pick(t, (256, 128)))
    h2, idx, gate = _peer_route(x2, w["g_ffn"], w["w_pq"], w["keys1"], w["keys2"],
                                _pick(t, (256, 128)))
    tm = _pick(t, (64, 32, 16, 8))
    coef = _peer_act(idx, h2.reshape(t, 8, 128), gate, w["utab"], tm)
    y3 = _peer_out(idx, coef, x2.reshape(t, 8, 128), g_final, w["vtab"], tm)
    return y3.reshape(b, s, D_MODEL)


def kernel(x_prompt, x_sample, mem_prompt, mem_sample, g_attn, w_in, g_qa, g_ka, sink_b, w_mkv,
           g_mem, w_pa, w_pb, w_pm, w_o, g_ffn, w_pq, peer_keys1, peer_keys2, peer_u, peer_v,
           g_final):
    assert g_attn.shape[0] == 1, "single-layer trunk"
    w_in_b = w_in[0].astype(BF16)
    w = dict(
        g_attn=g_attn[0], w_qkv=w_in_b[:, :QKV_W], w_gate=w_in_b[:, QKV_W:],
        g_qa=g_qa[0], g_ka=g_ka[0], sink_b=sink_b[0], w_mkv=w_mkv[0], g_mem=g_mem[0],
        w_pa=w_pa[0].astype(BF16), w_pb=w_pb[0].astype(BF16), w_pm=w_pm[0].astype(BF16),
        w_o=w_o[0].astype(BF16), g_ffn=g_ffn[0], w_pq=w_pq[0].astype(BF16),
        keys1=peer_keys1[0].astype(BF16), keys2=peer_keys2[0].astype(BF16),
        utab=_pack_table(peer_u[0]), vtab=_pack_table(peer_v[0]),
        slopes=2.0 ** (-8.0 * jnp.arange(1, Q_HEADS + 1, dtype=F32) / Q_HEADS),
    )
    return (_trunk(x_prompt, mem_prompt, w, g_final), _trunk(x_sample, mem_sample, w, g_final))
```

```python
import functools

import jax
import jax.numpy as jnp
import numpy as np
from jax import lax
from jax.experimental import pallas as pl
from jax.experimental.pallas import tpu as pltpu

F32 = jnp.float32
BF16 = jnp.bfloat16
I32 = jnp.int32

D_MODEL = 1024
HEAD_DIM = 64
Q_HEADS = 8
KV_HEADS = 2
GROUP = Q_HEADS // KV_HEADS
Q_W = Q_HEADS * HEAD_DIM
KV_W = KV_HEADS * HEAD_DIM
M_HEADS = 4
M_HEAD_DIM = 128
M_W = M_HEADS * M_HEAD_DIM
N_MEM = 256
QKV_W = 2 * (Q_W + 2 * KV_W) + M_W
GATE_W = 3 * D_MODEL
GRID_W = 64
WINDOW = 128
BLOCK = 128
SPAN = BLOCK + 2 * WINDOW
ROPE_THETA = 10000.0
ROPE_HALF = HEAD_DIM // 4
PEER_HEADS = 8
PEER_KEYS = 128
PEER_HALF = 128
PEER_TOPK = 16
PEER_PAIRS = PEER_HEADS * PEER_TOPK
EPS = 1e-6
NEG_INF = -1e30

VMEM_LIMIT_BYTES = 56 * 1024 * 1024
PACK_ROWS = 4
SAFE_LOGIT_BOUND = 40.0


def _cparams(n_axes):
    return pltpu.CompilerParams(
        dimension_semantics=("arbitrary",) * n_axes,
        vmem_limit_bytes=VMEM_LIMIT_BYTES,
    )


def _rms(x, g):
    ms = jnp.mean(x * x, axis=-1, keepdims=True)
    return (x * lax.rsqrt(ms + EPS)) * g


def _dot(a, b):
    return jnp.dot(a, b, preferred_element_type=F32)


def _dot_nt(a, b):
    return lax.dot_general(a, b, (((1,), (1,)), ((), ())), preferred_element_type=F32)


def _dot_f32_by_01(a, m01):
    hi = a.astype(BF16)
    r1 = a - hi.astype(F32)
    mid = r1.astype(BF16)
    lo = (r1 - mid.astype(F32)).astype(BF16)
    return _dot(lo, m01) + _dot(mid, m01) + _dot(hi, m01)


def _mem_kv_kernel(mem_ref, g_ref, w_ref, mkt_ref, mv_ref):
    h = _rms(mem_ref[0], g_ref[...]).astype(BF16)
    z = _dot(h, w_ref[...])
    mkt_ref[0] = z[:, :M_W].T.astype(BF16)
    mv_ref[0] = z[:, M_W:].astype(BF16)


def _mem_kv(mem, g_mem, w_mkv):
    b = mem.shape[0]
    return pl.pallas_call(
        _mem_kv_kernel,
        grid=(b,),
        in_specs=[
            pl.BlockSpec((1, N_MEM, D_MODEL), lambda i: (i, 0, 0)),
            pl.BlockSpec((1, D_MODEL), lambda i: (0, 0)),
            pl.BlockSpec((D_MODEL, 2 * M_W), lambda i: (0, 0)),
        ],
        out_specs=[
            pl.BlockSpec((1, M_W, N_MEM), lambda i: (i, 0, 0)),
            pl.BlockSpec((1, N_MEM, M_W), lambda i: (i, 0, 0)),
        ],
        out_shape=[
            jax.ShapeDtypeStruct((b, M_W, N_MEM), BF16),
            jax.ShapeDtypeStruct((b, N_MEM, M_W), BF16),
        ],
        compiler_params=_cparams(1),
        name="mem_kv",
    )(mem, g_mem.reshape(1, D_MODEL), w_mkv.astype(BF16))


def _head_norm_rope(q, g, bd, c, s_up, s_dn, width):
    ss = _dot_f32_by_01(q * q, bd)
    qn = (q * lax.rsqrt(ss * (1.0 / HEAD_DIM) + EPS)) * g
    up = pltpu.roll(qn, width - ROPE_HALF, 1)
    dn = pltpu.roll(qn, ROPE_HALF, 1)
    return qn * c + up * s_up + dn * s_dn


def _qkv_kernel(x_ref, g_ref, w_ref, gq_ref, gk_ref, c_ref, su_ref, sd_ref, bdq_ref, bdk_ref,
                qa_ref, kat_ref, va_ref, qb_ref, kb_ref, vb_ref, qm_ref):
    h = _rms(x_ref[0], g_ref[...]).astype(BF16)
    z = _dot(h, w_ref[...])
    c2, su2, sd2 = c_ref[...], su_ref[...], sd_ref[...]
    rep = Q_W // KV_W
    c8 = jnp.concatenate([c2] * rep, axis=1)
    su8 = jnp.concatenate([su2] * rep, axis=1)
    sd8 = jnp.concatenate([sd2] * rep, axis=1)
    o = 0
    qa = _head_norm_rope(z[:, o:o + Q_W], gq_ref[...], bdq_ref[...], c8, su8, sd8, Q_W)
    qa_ref[0] = (qa * (HEAD_DIM ** -0.5)).astype(BF16)
    o += Q_W
    ka = _head_norm_rope(z[:, o:o + KV_W], gk_ref[...], bdk_ref[...], c2, su2, sd2, KV_W)
    kat_ref[0] = ka.T.astype(BF16)
    o += KV_W
    va_ref[0] = z[:, o:o + KV_W].astype(BF16)
    o += KV_W
    qb_ref[0] = (z[:, o:o + Q_W] * (HEAD_DIM ** -0.5)).astype(BF16)
    o += Q_W
    kb_ref[0] = z[:, o:o + KV_W].astype(BF16)
    o += KV_W
    vb_ref[0] = z[:, o:o + KV_W].astype(BF16)
    o += KV_W
    qm_ref[0] = z[:, o:o + M_W].astype(BF16)


def _rope_tables(seq_len):
    rows = seq_len // GRID_W
    row_ids = jnp.repeat(jnp.arange(rows, dtype=F32), GRID_W)
    col_ids = jnp.tile(jnp.arange(GRID_W, dtype=F32), rows)
    inv_freq = ROPE_THETA ** (-jnp.arange(ROPE_HALF, dtype=F32) / ROPE_HALF)
    ang_r = row_ids[:, None] * inv_freq[None, :]
    ang_c = col_ids[:, None] * inv_freq[None, :]
    cr, sr, cc, sc = jnp.cos(ang_r), jnp.sin(ang_r), jnp.cos(ang_c), jnp.sin(ang_c)
    z = jnp.zeros_like(sr)
    c = jnp.concatenate([cr, cr, cc, cc], axis=1)
    s_up = jnp.concatenate([-sr, z, -sc, z], axis=1)
    s_dn = jnp.concatenate([z, sr, z, sc], axis=1)
    two = lambda t: jnp.concatenate([t, t], axis=1)
    return two(c), two(s_up), two(s_dn)


def _block_diag_ones(width):
    i = np.arange(width) // HEAD_DIM
    return jnp.asarray((i[:, None] == i[None, :]).astype(np.float32), dtype=BF16)


def _qkv_proj(x, g_attn, w_qkv, g_qa, g_ka, tm):
    b, s, _ = x.shape
    c, su, sd = _rope_tables(s)
    gq = jnp.tile(g_qa, Q_HEADS).reshape(1, Q_W)
    gk = jnp.tile(g_ka, KV_HEADS).reshape(1, KV_W)
    row = lambda w: pl.BlockSpec((1, tm, w), lambda i, j: (i, j, 0))
    tab = pl.BlockSpec((tm, KV_W), lambda i, j: (j, 0))
    const = lambda shp: pl.BlockSpec(shp, lambda i, j: (0,) * len(shp))
    sds = lambda w: jax.ShapeDtypeStruct((b, s, w), BF16)
    return pl.pallas_call(
        _qkv_kernel,
        grid=(b, s // tm),
        in_specs=[row(D_MODEL), const((1, D_MODEL)), const((D_MODEL, QKV_W)), const((1, Q_W)),
                  const((1, KV_W)), tab, tab, tab, const((Q_W, Q_W)), const((KV_W, KV_W))],
        out_specs=[row(Q_W), pl.BlockSpec((1, KV_W, tm), lambda i, j: (i, 0, j)), row(KV_W),
                   row(Q_W), row(KV_W), row(KV_W), row(M_W)],
        out_shape=[sds(Q_W), jax.ShapeDtypeStruct((b, KV_W, s), BF16), sds(KV_W),
                   sds(Q_W), sds(KV_W), sds(KV_W), sds(M_W)],
        compiler_params=_cparams(2),
        name="qkv_proj",
    )(x, g_attn.reshape(1, D_MODEL), w_qkv, gq, gk, c, su, sd,
      _block_diag_ones(Q_W), _block_diag_ones(KV_W))


def _kv_lane_mask(kvh, dtype):
    lane = lax.broadcasted_iota(I32, (1, KV_W), 1)
    return ((lane // HEAD_DIM) == kvh).astype(dtype)


def _attn_global_kernel(q_ref, kt_ref, v_ref, o_ref, q4_sc, m_sc, l_sc, acc_sc, k2_sm, *, seq, tk):
    tq = q_ref.shape[1]
    rows = GROUP * tq
    n_slab = tk // 128

    @pl.when(pl.program_id(1) == 0)
    def _():
        def key_norms(ci, best):
            off = pl.multiple_of(ci * tk, tk)
            kc = kt_ref[0, :, pl.ds(off, tk)].astype(F32)
            kk = kc * kc
            return tuple(jnp.maximum(best[h], jnp.sum(kk[h * HEAD_DIM:(h + 1) * HEAD_DIM],
                                                      axis=0, keepdims=True))
                         for h in range(KV_HEADS))

        zero = jnp.zeros((1, tk), F32)
        best = lax.fori_loop(0, seq // tk, key_norms, (zero,) * KV_HEADS)
        for h in range(KV_HEADS):
            k2_sm[h] = jnp.max(best[h], axis=1, keepdims=True)[0, 0]

    outs = [None] * Q_HEADS
    for kvh in range(KV_HEADS):
        for gi in range(GROUP):
            head = kvh * GROUP + gi
            qh = q_ref[0, :, head * HEAD_DIM:(head + 1) * HEAD_DIM]
            zeros = jnp.zeros_like(qh)
            q4_sc[gi * tq:(gi + 1) * tq, :] = jnp.concatenate(
                [qh, zeros] if kvh == 0 else [zeros, qh], axis=1)
        q4 = q4_sc[...].astype(F32)
        q2 = jnp.max(jnp.sum(q4 * q4, axis=1, keepdims=True), axis=0, keepdims=True)[0, 0]
        bound = jnp.sqrt(q2 * k2_sm[kvh])

        @pl.when(bound <= SAFE_LOGIT_BOUND)
        def _():
            m_sc[...] = jnp.full((rows, 128), bound, F32)

        @pl.when(bound > SAFE_LOGIT_BOUND)
        def _():
            m_sc[...] = jnp.full((rows, 128), -jnp.inf, F32)

            def row_max(ci, carry):
                off = pl.multiple_of(ci * tk, tk)
                s = _dot(q4_sc[...], kt_ref[0, :, pl.ds(off, tk)])
                m = m_sc[...]
                for j in range(n_slab):
                    m = jnp.maximum(m, s[:, j * 128:(j + 1) * 128])
                m_sc[...] = m
                return carry

            lax.fori_loop(0, seq // tk, row_max, 0)
            m_sc[...] = jnp.broadcast_to(jnp.max(m_sc[...], axis=-1, keepdims=True), (rows, 128))

        l_sc[...] = jnp.zeros((rows, 128), F32)
        acc_sc[...] = jnp.zeros((rows, KV_W), F32)

        def accumulate(ci, carry):
            off = pl.multiple_of(ci * tk, tk)
            s = _dot(q4_sc[...], kt_ref[0, :, pl.ds(off, tk)])
            m = m_sc[...]
            l = l_sc[...]
            ps = []
            for j in range(n_slab):
                pj = jnp.exp(s[:, j * 128:(j + 1) * 128] - m)
                l = l + pj
                ps.append(pj.astype(BF16))
            l_sc[...] = l
            acc_sc[...] += _dot(jnp.concatenate(ps, axis=1), v_ref[0, pl.ds(off, tk), :])
            return carry

        lax.fori_loop(0, seq // tk, accumulate, 0)
        o = acc_sc[...] / jnp.sum(l_sc[...], axis=-1, keepdims=True)
        for gi in range(GROUP):
            outs[kvh * GROUP + gi] = o[gi * tq:(gi + 1) * tq, kvh * HEAD_DIM:(kvh + 1) * HEAD_DIM]
    o_ref[0] = jnp.concatenate(outs, axis=1).astype(BF16)


def _attn_global(qa, kat, va, tq, tk):
    b, s, _ = qa.shape
    return pl.pallas_call(
        functools.partial(_attn_global_kernel, seq=s, tk=tk),
        grid=(b, s // tq),
        in_specs=[
            pl.BlockSpec((1, tq, Q_W), lambda i, j: (i, j, 0)),
            pl.BlockSpec((1, KV_W, s), lambda i, j: (i, 0, 0)),
            pl.BlockSpec((1, s, KV_W), lambda i, j: (i, 0, 0)),
        ],
        out_specs=pl.BlockSpec((1, tq, Q_W), lambda i, j: (i, j, 0)),
        out_shape=jax.ShapeDtypeStruct((b, s, Q_W), BF16),
        scratch_shapes=[pltpu.VMEM((GROUP * tq, KV_W), BF16), pltpu.VMEM((GROUP * tq, 128), F32),
                        pltpu.VMEM((GROUP * tq, 128), F32), pltpu.VMEM((GROUP * tq, KV_W), F32),
                        pltpu.SMEM((KV_HEADS,), F32)],
        compiler_params=_cparams(2),
        name="attn_global",
    )(qa, kat, va)


def _attn_window_kernel(slope_ref, sink_ref, q_ref, k_ref, v_ref, o_ref, *, seq):
    i = pl.program_id(1)
    start = pl.multiple_of(i * BLOCK, BLOCK)
    ks = k_ref[0, pl.ds(start, SPAN), :]
    vs = v_ref[0, pl.ds(start, SPAN), :]
    r = lax.broadcasted_iota(I32, (BLOCK, SPAN), 0)
    c = lax.broadcasted_iota(I32, (BLOCK, SPAN), 1)
    dist = r + WINDOW - c
    adist = jnp.abs(dist)
    key_pos = start - WINDOW + c
    valid = (adist <= WINDOW) & (key_pos >= 0) & (key_pos < seq)
    adist_f = adist.astype(F32)
    outs = [None] * Q_HEADS
    for kvh in range(KV_HEADS):
        heads = range(kvh * GROUP, (kvh + 1) * GROUP)
        qz = []
        for head in heads:
            qh = q_ref[0, :, head * HEAD_DIM:(head + 1) * HEAD_DIM]
            zeros = jnp.zeros_like(qh)
            qz.append(jnp.concatenate([qh, zeros] if kvh == 0 else [zeros, qh], axis=1))
        s = _dot_nt(jnp.concatenate(qz, axis=0), ks)
        logits = jnp.concatenate(
            [jnp.where(valid, s[g * BLOCK:(g + 1) * BLOCK] - slope_ref[head] * adist_f, NEG_INF)
             for g, head in enumerate(heads)], axis=0)
        sink = jnp.concatenate([jnp.full((BLOCK, 1), sink_ref[head], F32) for head in heads], axis=0)
        m = jnp.maximum(jnp.max(logits, axis=-1, keepdims=True), sink)
        e = jnp.exp(logits - m)
        denom = jnp.sum(e, axis=-1, keepdims=True) + jnp.exp(sink - m)
        o = _dot((e / denom).astype(BF16), vs)
        for g, head in enumerate(heads):
            outs[head] = o[g * BLOCK:(g + 1) * BLOCK, kvh * HEAD_DIM:(kvh + 1) * HEAD_DIM]
    o_ref[0] = jnp.concatenate(outs, axis=1).astype(BF16)


def _attn_window(qb, kb, vb, slopes, sink):
    b, s, _ = qb.shape
    pad = ((0, 0), (WINDOW, WINDOW), (0, 0))
    kp, vp = jnp.pad(kb, pad), jnp.pad(vb, pad)
    smem = pl.BlockSpec(memory_space=pltpu.SMEM)
    return pl.pallas_call(
        functools.partial(_attn_window_kernel, seq=s),
        grid=(b, s // BLOCK),
        in_specs=[
            smem, smem,
            pl.BlockSpec((1, BLOCK, Q_W), lambda i, j: (i, j, 0)),
            pl.BlockSpec((1, s + 2 * WINDOW, KV_W), lambda i, j: (i, 0, 0)),
            pl.BlockSpec((1, s + 2 * WINDOW, KV_W), lambda i, j: (i, 0, 0)),
        ],
        out_specs=pl.BlockSpec((1, BLOCK, Q_W), lambda i, j: (i, j, 0)),
        out_shape=jax.ShapeDtypeStruct((b, s, Q_W), BF16),
        compiler_params=_cparams(2),
        name="attn_window",
    )(slopes, sink, qb, kp, vp)


def _attn_mem_kernel(q_ref, mkt_ref, mv_ref, o_ref):
    scale = M_HEAD_DIM ** -0.5
    outs = []
    for head in range(M_HEADS):
        sl = slice(head * M_HEAD_DIM, (head + 1) * M_HEAD_DIM)
        logits = _dot(q_ref[0, :, sl], mkt_ref[0, sl, :]) * scale
        m = jnp.max(logits, axis=-1, keepdims=True)
        e = jnp.exp(logits - m)
        p = (e / jnp.sum(e, axis=-1, keepdims=True)).astype(BF16)
        outs.append(_dot(p, mv_ref[0, :, sl]))
    o_ref[0] = jnp.concatenate(outs, axis=1).astype(BF16)


def _attn_mem(qm, mkt, mv, tq):
    b, s, _ = qm.shape
    return pl.pallas_call(
        _attn_mem_kernel,
        grid=(b, s // tq),
        in_specs=[
            pl.BlockSpec((1, tq, M_W), lambda i, j: (i, j, 0)),
            pl.BlockSpec((1, M_W, N_MEM), lambda i, j: (i, 0, 0)),
            pl.BlockSpec((1, N_MEM, M_W), lambda i, j: (i, 0, 0)),
        ],
        out_specs=pl.BlockSpec((1, tq, M_W), lambda i, j: (i, j, 0)),
        out_shape=jax.ShapeDtypeStruct((b, s, M_W), BF16),
        compiler_params=_cparams(2),
        name="attn_mem",
    )(qm, mkt, mv)


def _merge_kernel(x_ref, g_ref, wg_ref, oa_ref, ob_ref, om_ref, wpa_ref, wpb_ref, wpm_ref, wo_ref,
                  x2_ref):
    x = x_ref[...]
    h = _rms(x, g_ref[...]).astype(BF16)
    gates = jax.nn.sigmoid(_dot(h, wg_ref[...]))
    merged = (gates[:, :D_MODEL] * _dot(oa_ref[...], wpa_ref[...])
              + gates[:, D_MODEL:2 * D_MODEL] * _dot(ob_ref[...], wpb_ref[...])
              + gates[:, 2 * D_MODEL:] * _dot(om_ref[...], wpm_ref[...]))
    x2_ref[...] = x + _dot(merged.astype(BF16), wo_ref[...])


def _merge_proj(x, g_attn, w_gate, oa, ob, om, w_pa, w_pb, w_pm, w_o, tm):
    t = x.shape[0]
    row = lambda w: pl.BlockSpec((tm, w), lambda i: (i, 0))
    const = lambda shp: pl.BlockSpec(shp, lambda i: (0, 0))
    return pl.pallas_call(
        _merge_kernel,
        grid=(t // tm,),
        in_specs=[row(D_MODEL), const((1, D_MODEL)), const((D_MODEL, GATE_W)),
                  row(Q_W), row(Q_W), row(M_W),
                  const((Q_W, D_MODEL)), const((Q_W, D_MODEL)), const((M_W, D_MODEL)),
                  const((D_MODEL, D_MODEL))],
        out_specs=row(D_MODEL),
        out_shape=jax.ShapeDtypeStruct((t, D_MODEL), F32),
        compiler_params=_cparams(1),
        name="merge_proj",
    )(x, g_attn.reshape(1, D_MODEL), w_gate, oa, ob, om, w_pa, w_pb, w_pm, w_o)


def _topk_rows(s, aux, k):
    n = s.shape[0]
    iota = lax.broadcasted_iota(I32, s.shape, 0).astype(F32)
    vals, picks = [], []
    for _ in range(k):
        m = jnp.max(s, axis=0, keepdims=True)
        win = jnp.min(jnp.where(s == m, iota, float(n)), axis=0, keepdims=True)
        sel = iota == win
        vals.append(m)
        if aux is None:
            picks.append(win)
        else:
            picks.append(jnp.max(jnp.where(sel, aux, -1.0), axis=0, keepdims=True))
        s = jnp.where(sel, -jnp.inf, s)
    return jnp.concatenate(vals, axis=0), jnp.concatenate(picks, axis=0)


def _pair_candidates(a, b, combine):
    half = PEER_TOPK // 2
    rows = [combine(a[0:1], b)]
    rows += [combine(a[i:i + 1], b[0:half]) for i in range(1, half)]
    rows.append(combine(a[half:], b[0:1]))
    return jnp.concatenate(rows, axis=0)


def _route_kernel(x2_ref, g_ref, wq_ref, k1_ref, k2_ref, h2_ref, idx_ref, gate_ref):
    h2 = _rms(x2_ref[...], g_ref[...])
    h2_ref[...] = h2
    q = _dot(h2.astype(BF16), wq_ref[...]).astype(BF16)
    idx_rows, gate_rows = [], []
    for head in range(PEER_HEADS):
        o = head * 2 * PEER_HALF
        s1 = _dot_nt(k1_ref[head], q[:, o:o + PEER_HALF])
        s2 = _dot_nt(k2_ref[head], q[:, o + PEER_HALF:o + 2 * PEER_HALF])
        sv1, si1 = _topk_rows(s1, None, PEER_TOPK)
        sv2, si2 = _topk_rows(s2, None, PEER_TOPK)
        cand = _pair_candidates(sv1, sv2, lambda a, b: a + b)
        cidx = _pair_candidates(si1, si2, lambda a, b: a * float(PEER_KEYS) + b)
        score, eidx = _topk_rows(cand, cidx, PEER_TOPK)
        e = jnp.exp(score - score[0:1])
        gate_rows.append(e / jnp.sum(e, axis=0, keepdims=True))
        idx_rows.append(eidx)
    idx = (jnp.concatenate(idx_rows, axis=0) * float(PACK_ROWS)).astype(I32)
    idx_ref[...] = idx.T
    gate_ref[...] = jnp.concatenate(gate_rows, axis=0).T


def _peer_route(x2, g_ffn, w_pq, keys1, keys2, tm):
    t = x2.shape[0]
    row = lambda w: pl.BlockSpec((tm, w), lambda i: (i, 0))
    const = lambda shp: pl.BlockSpec(shp, lambda i: (0,) * len(shp))
    kshape = (PEER_HEADS, PEER_KEYS, PEER_HALF)
    return pl.pallas_call(
        _route_kernel,
        grid=(t // tm,),
        in_specs=[row(D_MODEL), const((1, D_MODEL)), const((D_MODEL, PEER_HEADS * 2 * PEER_HALF)),
                  const(kshape), const(kshape)],
        out_specs=[row(D_MODEL), row(PEER_PAIRS), row(PEER_PAIRS)],
        out_shape=[jax.ShapeDtypeStruct((t, D_MODEL), F32),
                   jax.ShapeDtypeStruct((t, PEER_PAIRS), I32),
                   jax.ShapeDtypeStruct((t, PEER_PAIRS), F32)],
        compiler_params=_cparams(1),
        name="peer_route",
    )(x2, g_ffn.reshape(1, D_MODEL), w_pq, keys1, keys2)


def _pack_table(tab):
    e = tab.shape[0]
    bits = lax.bitcast_convert_type(tab.astype(BF16), jnp.uint16).astype(jnp.uint32)
    bits = bits.reshape(e, PACK_ROWS, 2, 128)
    words = bits[:, :, 0, :] | (bits[:, :, 1, :] << 16)
    return lax.bitcast_convert_type(words, I32).reshape(e * PACK_ROWS, 128)


def _table_spec(rows):
    return pl.BlockSpec((rows, 128), lambda i: (0, 0), pipeline_mode=pl.Buffered(1))


ID_GROUP = 8
ID_SLOT = ID_GROUP * PEER_PAIRS


def _for_id_groups(idx_hbm, idx_sm, sem, tm, n_steps, body):
    n_groups = tm // ID_GROUP
    step = pl.program_id(0)
    last_group = n_steps * n_groups - 1
    first = step * n_groups

    def copy(g, slot):
        src = idx_hbm.at[pl.ds(g * ID_SLOT, ID_SLOT)]
        return pltpu.make_async_copy(src, idx_sm[slot], sem.at[slot])

    @pl.when(step == 0)
    def _():
        copy(0, 0).start()

    def group_pair(j, carry):
        g0 = first + 2 * j
        copy(g0 + 1, 1).start()
        copy(g0, 0).wait()
        body(2 * j * ID_GROUP, idx_sm[0])
        copy(jnp.minimum(g0 + 2, last_group), 0).start()
        copy(g0 + 1, 1).wait()
        body((2 * j + 1) * ID_GROUP, idx_sm[1])
        return carry

    lax.fori_loop(0, n_groups // 2, group_pair, 0)

    @pl.when(step == n_steps - 1)
    def _():
        copy(last_group, 0).wait()


def _id_scratch():
    return [pltpu.SMEM((ID_SLOT,), I32), pltpu.SMEM((ID_SLOT,), I32), pltpu.SemaphoreType.DMA((2,))]


def _peer_act_kernel(idx_hbm, h3_ref, gate_ref, tab_ref, coef_ref, part_a, part_b, act_sc,
                     ids_a, ids_b, sem, *, n_steps):
    tm = h3_ref.shape[0]

    def group(t0, ids):
        for s in range(ID_GROUP):
            part_sc = part_b if s % 2 else part_a
            x = h3_ref[t0 + s].astype(BF16)
            for k in range(PEER_PAIRS):
                row = pl.multiple_of(ids[s * PEER_PAIRS + k], PACK_ROWS)
                w = pltpu.bitcast(tab_ref[pl.ds(row, PACK_ROWS), :], BF16)
                prod = (w * x).astype(F32)
                part_sc[k * PACK_ROWS:(k + 1) * PACK_ROWS, :] = prod[0:PACK_ROWS] + prod[PACK_ROWS:]
            part = part_sc[pl.ds(0, PEER_PAIRS, stride=PACK_ROWS), :]
            for r in range(1, PACK_ROWS):
                part = part + part_sc[pl.ds(r, PEER_PAIRS, stride=PACK_ROWS), :]
            act_sc[pl.ds(t0 + s, 1), :] = jnp.sum(part.T, axis=0, keepdims=True)

    _for_id_groups(idx_hbm, (ids_a, ids_b), sem, tm, n_steps, group)
    a = act_sc[...]
    gelu = 0.5 * a * (1.0 + lax.erf(a * (2.0 ** -0.5)))
    coef_ref[...] = gelu * gate_ref[...]


def _peer_act(idx, h3, gate, utab, tm):
    t = h3.shape[0]
    return pl.pallas_call(
        functools.partial(_peer_act_kernel, n_steps=t // tm),
        grid=(t // tm,),
        in_specs=[
            pl.BlockSpec(memory_space=pl.ANY),
            pl.BlockSpec((tm, 8, 128), lambda i: (i, 0, 0)),
            pl.BlockSpec((tm, PEER_PAIRS), lambda i: (i, 0)),
            _table_spec(utab.shape[0]),
        ],
        out_specs=pl.BlockSpec((tm, PEER_PAIRS), lambda i: (i, 0)),
        out_shape=jax.ShapeDtypeStruct((t, PEER_PAIRS), F32),
        scratch_shapes=[pltpu.VMEM((PEER_PAIRS * PACK_ROWS, 128), F32),
                        pltpu.VMEM((PEER_PAIRS * PACK_ROWS, 128), F32),
                        pltpu.VMEM((tm, PEER_PAIRS), F32)] + _id_scratch(),
        compiler_params=_cparams(1),
        name="peer_act",
    )(idx.reshape(-1), h3, gate, utab)


def _peer_out_kernel(idx_hbm, coef_ref, x3_ref, g_ref, tab_ref, y_ref, o_sc, cb_a, cb_b,
                     ids_a, ids_b, sem, *, n_steps):
    tm = x3_ref.shape[0]
    n_acc = 4

    def spread(t, cb_sc):
        c = coef_ref[pl.ds(t, 1), :].astype(BF16).astype(F32)
        bits = lax.bitcast_convert_type(c, I32)
        words = bits | lax.shift_right_logical(bits, 16)
        cb_sc[...] = jnp.broadcast_to(words, (PEER_PAIRS, PEER_PAIRS)).T

    def group(t0, ids):
        spread(t0, cb_a)
        for s in range(ID_GROUP):
            cb_sc = cb_b if s % 2 else cb_a
            if s + 1 < ID_GROUP:
                spread(t0 + s + 1, cb_a if s % 2 else cb_b)
            acc = [jnp.zeros((8, 128), F32) for _ in range(n_acc)]
            for k in range(PEER_PAIRS):
                row = pl.multiple_of(ids[s * PEER_PAIRS + k], PACK_ROWS)
                w = pltpu.bitcast(tab_ref[pl.ds(row, PACK_ROWS), :], BF16)
                c = pltpu.bitcast(jnp.broadcast_to(cb_sc[k:k + 1, :], (PACK_ROWS, 128)), BF16)
                acc[k % n_acc] = acc[k % n_acc] + (c * w).astype(F32)
            o_sc[t0 + s] = (acc[0] + acc[1]) + (acc[2] + acc[3])

    _for_id_groups(idx_hbm, (ids_a, ids_b), sem, tm, n_steps, group)
    z = x3_ref[...] + o_sc[...]
    ss = jnp.sum(jnp.sum(z * z, axis=2, keepdims=True), axis=1, keepdims=True)
    y_ref[...] = (z * lax.rsqrt(ss * (1.0 / D_MODEL) + EPS)) * g_ref[...]


def _peer_out(idx, coef, x3, g_final, vtab, tm):
    t = x3.shape[0]
    return pl.pallas_call(
        functools.partial(_peer_out_kernel, n_steps=t // tm),
        grid=(t // tm,),
        in_specs=[
            pl.BlockSpec(memory_space=pl.ANY),
            pl.BlockSpec((tm, PEER_PAIRS), lambda i: (i, 0)),
            pl.BlockSpec((tm, 8, 128), lambda i: (i, 0, 0)),
            pl.BlockSpec((1, 8, 128), lambda i: (0, 0, 0)),
            _table_spec(vtab.shape[0]),
        ],
        out_specs=pl.BlockSpec((tm, 8, 128), lambda i: (i, 0, 0)),
        out_shape=jax.ShapeDtypeStruct((t, 8, 128), F32),
        scratch_shapes=[pltpu.VMEM((tm, 8, 128), F32), pltpu.VMEM((PEER_PAIRS, PEER_PAIRS), I32),
                        pltpu.VMEM((PEER_PAIRS, PEER_PAIRS), I32)] + _id_scratch(),
        compiler_params=_cparams(1),
        name="peer_out",
    )(idx.reshape(-1), coef, x3, g_final.reshape(1, 8, 128), vtab)


def _pick(n, prefs):
    for p in prefs:
        if n % p == 0:
            return p
    raise ValueError(f"no tile in {prefs} divides {n}")


def _trunk(x, mem, w, g_final):
    b, s, _ = x.shape
    t = b * s
    mkt, mv = _mem_kv(mem, w["g_mem"], w["w_mkv"])
    qa, kat, va, qb, kb, vb, qm = _qkv_proj(x, w["g_attn"], w["w_qkv"], w["g_qa"], w["g_ka"],
                                           _pick(s, (512, 256, 128)))
    oa = _attn_global(qa, kat, va, _pick(s, (512, 256, 128)), _pick(s, (1024, 512, 256, 128)))
    ob = _attn_window(qb, kb, vb, w["slopes"], w["sink_b"])
    om = _attn_mem(qm, mkt, mv, _pick(s, (512, 256, 128)))
    flat = lambda a: a.reshape(t, a.shape[-1])
    x2 = _merge_proj(flat(x), w["g_attn"], w["w_gate"], flat(oa), flat(ob), flat(om),
                     w["w_pa"], w["w_pb"], w["w_pm"], w["w_o"], _pick(t, (256, 128)))
    h2, idx, gate = _peer_route(x2, w["g_ffn"], w["w_pq"], w["keys1"], w["keys2"],
                                _pick(t, (256, 128)))
    tm = _pick(t, (128, 64, 32, 16))
    coef = _peer_act(idx, h2.reshape(t, 8, 128), gate, w["utab"], tm)
    y3 = _peer_out(idx, coef, x2.reshape(t, 8, 128), g_final, w["vtab"], tm)
    return y3.reshape(b, s, D_MODEL)


def kernel(x_prompt, x_sample, mem_prompt, mem_sample, g_attn, w_in, g_qa, g_ka, sink_b, w_mkv,
           g_mem, w_pa, w_pb, w_pm, w_o, g_ffn, w_pq, peer_keys1, peer_keys2, peer_u, peer_v,
           g_final):
    assert g_attn.shape[0] == 1, "single-layer trunk"
    w_in_b = w_in[0].astype(BF16)
    w = dict(
        g_attn=g_attn[0], w_qkv=w_in_b[:, :QKV_W], w_gate=w_in_b[:, QKV_W:],
        g_qa=g_qa[0], g_ka=g_ka[0], sink_b=sink_b[0], w_mkv=w_mkv[0], g_mem=g_mem[0],
        w_pa=w_pa[0].astype(BF16), w_pb=w_pb[0].astype(BF16), w_pm=w_pm[0].astype(BF16),
        w_o=w_o[0].astype(BF16), g_ffn=g_ffn[0], w_pq=w_pq[0].astype(BF16),
        keys1=peer_keys1[0].astype(BF16), keys2=peer_keys2[0].astype(BF16),
        utab=_pack_table(peer_u[0]), vtab=_pack_table(peer_v[0]),
        slopes=2.0 ** (-8.0 * jnp.arange(1, Q_HEADS + 1, dtype=F32) / Q_HEADS),
    )
    return (_trunk(x_prompt, mem_prompt, w, g_final), _trunk(x_sample, mem_sample, w, g_final))
```
